```python
import math
import jax, jax.numpy as jnp
from jax import lax
import numpy as np

D_MODEL = 1024
BATCH = 4
SEQ = 4096
DEPTH = 2

CTX_LEN = 256
GRID_W = 64
HEAD_DIM = 64
A_GROUPS = 4
A_WIDTH = A_GROUPS * HEAD_DIM
MLP_CHUNK = 128
ATT_Q_HEADS = 6
ATT_KV_HEADS = 2
ATT_GROUP = ATT_Q_HEADS // ATT_KV_HEADS
ATT_WIDTH = ATT_Q_HEADS * HEAD_DIM
KV_WIDTH = ATT_KV_HEADS * HEAD_DIM
Q_BLOCK = 128
ROPE_THETA = 10000.0
DN_HEADS = 6
DN_WIDTH = DN_HEADS * HEAD_DIM
DN_CONV = 5
DN_CHUNK = 64
MIX_WIDTH = A_WIDTH + ATT_WIDTH + DN_WIDTH
A_COLS = 2 * A_WIDTH
B_COLS = ATT_WIDTH + 2 * KV_WIDTH
C_COLS = 4 * DN_WIDTH + 4 * DN_HEADS
IN_COLS = A_COLS + B_COLS + C_COLS
D_FF = 4 * D_MODEL
DEEPNORM_ALPHA = (2 * DEPTH) ** 0.25
DEEPNORM_BETA = (8 * DEPTH) ** -0.25
EPS = 1e-6

kernel_name = "hybrid_gmlp_gqa_deltanet_dit_block"


def _norm(x):
    xf = x.astype(jnp.float32)
    mu = jnp.mean(xf, -1, keepdims=True)
    var = jnp.mean(jnp.square(xf - mu), -1, keepdims=True)
    return ((xf - mu) * lax.rsqrt(var + EPS)).astype(x.dtype)


def layer_norm(x, g, b):
    return _norm(x) * g + b


def rms_norm(x, g):
    xf = x.astype(jnp.float32)
    return (xf * lax.rsqrt(jnp.mean(xf * xf, -1, keepdims=True) + EPS)).astype(x.dtype) * g


def l2_normalize(x):
    xf = x.astype(jnp.float32)
    return (xf * lax.rsqrt(jnp.sum(xf * xf, -1, keepdims=True) + EPS)).astype(x.dtype)


def modulate(x, shift, scale):
    return _norm(x) * (1.0 + scale) + shift


def axial_rope_angles(rows):
    row = jnp.repeat(jnp.arange(rows, dtype=jnp.float32), GRID_W)
    col = jnp.tile(jnp.arange(GRID_W, dtype=jnp.float32), rows)
    half = HEAD_DIM // 2
    inv = 1.0 / (ROPE_THETA ** (jnp.arange(0, half, 2, dtype=jnp.float32) / half))
    return row[:, None] * inv, col[:, None] * inv


def _rotate(x, ang):
    cos = jnp.cos(ang)[None, :, None, :].astype(x.dtype)
    sin = jnp.sin(ang)[None, :, None, :].astype(x.dtype)
    x1, x2 = jnp.split(x, 2, -1)
    return jnp.concatenate([x1 * cos - x2 * sin, x2 * cos + x1 * sin], -1)


def apply_axial_rope(x, ang_row, ang_col):
    xr, xc = jnp.split(x, 2, -1)
    return jnp.concatenate([_rotate(xr, ang_row), _rotate(xc, ang_col)], -1)


def chunk_token_mlp(p, ln_g, ln_b, w_s, b_s):
    bsz, n, _ = p.shape
    u, v = jnp.split(jax.nn.gelu(p, approximate=False), 2, -1)
    v = layer_norm(v, ln_g, ln_b).reshape(bsz, n // MLP_CHUNK, MLP_CHUNK, A_GROUPS, HEAD_DIM)
    mixed = jnp.einsum('bcsgd,gts->bctgd', v, w_s) + b_s.T[None, None, :, :, None]
    return u * mixed.reshape(bsz, n, A_WIDTH)


def _attend(q, k, v):
    s = jnp.einsum('bqhgd,bkhd->bhgqk', q, k).astype(jnp.float32) * (HEAD_DIM ** -0.5)
    pr = jax.nn.softmax(s, axis=-1).astype(v.dtype)
    return jnp.einsum('bhgqk,bkhd->bqhgd', pr, v)


def gqa_mixer(p_ctx, p_lat, q_g, k_g, ang_row, ang_col, need_ctx):
    def split(p):
        bsz, n, _ = p.shape
        q = p[..., :ATT_WIDTH].reshape(bsz, n, ATT_Q_HEADS, HEAD_DIM)
        k = p[..., ATT_WIDTH:ATT_WIDTH + KV_WIDTH].reshape(bsz, n, ATT_KV_HEADS, HEAD_DIM)
        v = p[..., ATT_WIDTH + KV_WIDTH:].reshape(bsz, n, ATT_KV_HEADS, HEAD_DIM)
        return rms_norm(q, q_g), rms_norm(k, k_g), v

    qc, kc, vc = split(p_ctx)
    ql, kl, vl = split(p_lat)
    ql = apply_axial_rope(ql, ang_row, ang_col)
    kl = apply_axial_rope(kl, ang_row, ang_col)
    bsz, n = p_lat.shape[:2]
    k_all = jnp.concatenate([kl, kc], 1)
    v_all = jnp.concatenate([vl, vc], 1)
    qb = ql.reshape(bsz, n // Q_BLOCK, Q_BLOCK, ATT_KV_HEADS, ATT_GROUP, HEAD_DIM).transpose(1, 0, 2, 3, 4, 5)
    ol = lax.map(lambda qblk: _attend(qblk, k_all, v_all), qb)
    ol = ol.transpose(1, 0, 2, 3, 4, 5).reshape(bsz, n, ATT_WIDTH)
    oc = None
    if need_ctx:
        nc = p_ctx.shape[1]
        oc = _attend(qc.reshape(bsz, nc, ATT_KV_HEADS, ATT_GROUP, HEAD_DIM), kc, vc).reshape(bsz, nc, ATT_WIDTH)
    return oc, ol


def short_conv(x, w):
    out = lax.conv_general_dilated(x, w[:, None, :], window_strides=(1,),
                                   padding=[(DN_CONV // 2, DN_CONV // 2)],
                                   dimension_numbers=('NWC', 'WIO', 'NWC'),
                                   feature_group_count=x.shape[-1])
    return jax.nn.silu(out)


def gated_delta_rule(q, k, v, g, beta, state):
    bsz, n, h, d = q.shape
    nc = n // DN_CHUNK

    def chunks(t):
        t = jnp.moveaxis(t.astype(jnp.float32), 2, 1)
        return t.reshape(bsz, h, nc, DN_CHUNK, *t.shape[3:])

    qf, kf, vf = chunks(q), chunks(k), chunks(v)
    gc = jnp.cumsum(chunks(g), -1)
    bf = chunks(beta)
    incl = jnp.tril(jnp.ones((DN_CHUNK, DN_CHUNK), bool))
    strict = jnp.tril(jnp.ones((DN_CHUNK, DN_CHUNK), bool), -1)
    diff = gc[..., :, None] - gc[..., None, :]
    decay = jnp.where(incl, jnp.exp(jnp.minimum(diff, 0.0)), 0.0)
    kb = kf * bf[..., None]
    lower = jnp.where(strict, jnp.einsum('bhcid,bhcjd->bhcij', kb, kf) * decay, 0.0)
    eye = jnp.eye(DN_CHUNK, dtype=jnp.float32)
    rhs = jnp.concatenate([vf * bf[..., None], kb * jnp.exp(gc)[..., None]], -1)
    sol = lax.linalg.triangular_solve(eye + lower, rhs, left_side=True, lower=True, unit_diagonal=True)
    u, w = jnp.split(sol, 2, -1)
    intra = jnp.where(incl, jnp.einsum('bhcid,bhcjd->bhcij', qf, kf) * decay, 0.0)
    g_last = gc[..., -1]
    q_dec = qf * jnp.exp(gc)[..., None]
    k_dec = kf * jnp.exp(g_last[..., None] - gc)[..., None]
    xs = tuple(jnp.moveaxis(t, 2, 0) for t in (q_dec, k_dec, u, w, intra, g_last))

    def step(s, inp):
        qd, kd, u_i, w_i, a_i, gl = inp
        v_new = u_i - jnp.einsum('bhck,bhkv->bhcv', w_i, s)
        o = jnp.einsum('bhck,bhkv->bhcv', qd, s) + jnp.einsum('bhij,bhjv->bhiv', a_i, v_new)
        s = s * jnp.exp(gl)[..., None, None] + jnp.einsum('bhck,bhcv->bhkv', kd, v_new)
        return s, o

    s_final, o = lax.scan(step, state, xs)
    o = jnp.moveaxis(o, 0, 2).reshape(bsz, h, n, d)
    return jnp.moveaxis(o, 1, 2).astype(v.dtype), s_final


def deltanet_mixer(p_ctx, p_lat, conv_w, a_log, dt_bias, norm_g, need_ctx):
    def prep(p):
        bsz, n, _ = p.shape
        qkv = short_conv(p[..., :3 * DN_WIDTH], conv_w)
        q, k, v = jnp.split(qkv, 3, -1)
        q = l2_normalize(q.reshape(bsz, n, DN_HEADS, HEAD_DIM)) * (HEAD_DIM ** -0.5)
        k = l2_normalize(k.reshape(bsz, n, DN_HEADS, HEAD_DIM))
        v = v.reshape(bsz, n, DN_HEADS, HEAD_DIM)
        z = p[..., 3 * DN_WIDTH:4 * DN_WIDTH]
        a = p[..., 4 * DN_WIDTH:4 * DN_WIDTH + 2 * DN_HEADS].reshape(bsz, n, 2, DN_HEADS).astype(jnp.float32)
        bt = p[..., 4 * DN_WIDTH + 2 * DN_HEADS:].reshape(bsz, n, 2, DN_HEADS).astype(jnp.float32)
        g = -jnp.exp(a_log.astype(jnp.float32)) * jax.nn.softplus(a + dt_bias.astype(jnp.float32))
        return q, k, v, z, g, jax.nn.sigmoid(bt)

    def out_gate(o, z):
        bsz, n = z.shape[:2]
        zz = z.reshape(bsz, n, DN_HEADS, HEAD_DIM)
        return (rms_norm(o, norm_g) * jax.nn.silu(zz)).reshape(bsz, n, DN_WIDTH)

    flip = lambda t: jnp.flip(t, 1)
    qc, kc, vc, zc, gc, bc = prep(p_ctx)
    ql, kl, vl, zl, gl, bl = prep(p_lat)
    bsz = p_lat.shape[0]
    zero = jnp.zeros((bsz, DN_HEADS, HEAD_DIM, HEAD_DIM), jnp.float32)
    oc_f, s_f = gated_delta_rule(qc, kc, vc, gc[:, :, 0], bc[:, :, 0], zero)
    oc_b, s_b = gated_delta_rule(flip(qc), flip(kc), flip(vc), flip(gc[:, :, 1]), flip(bc[:, :, 1]), zero)
    ol_f, _ = gated_delta_rule(ql, kl, vl, gl[:, :, 0], bl[:, :, 0], s_f)
    ol_b, _ = gated_delta_rule(flip(ql), flip(kl), flip(vl), flip(gl[:, :, 1]), flip(bl[:, :, 1]), s_b)
    ol = out_gate(ol_f + flip(ol_b), zl)
    oc = out_gate(oc_f + flip(oc_b), zc) if need_ctx else None
    return oc, ol


def token_mixers(p_ctx, p_lat, ang_row, ang_col, gmlp_ln_g, gmlp_ln_b, gmlp_w_s, gmlp_b_s,
                 attn_q_g, attn_k_g, dn_conv_w, dn_a_log, dn_dt_bias, dn_norm_g, need_ctx):
    s_a, s_b = A_COLS, A_COLS + B_COLS
    ya_l = chunk_token_mlp(p_lat[..., :s_a], gmlp_ln_g, gmlp_ln_b, gmlp_w_s, gmlp_b_s)
    yb_c, yb_l = gqa_mixer(p_ctx[..., s_a:s_b], p_lat[..., s_a:s_b], attn_q_g, attn_k_g, ang_row, ang_col, need_ctx)
    yc_c, yc_l = deltanet_mixer(p_ctx[..., s_b:], p_lat[..., s_b:], dn_conv_w, dn_a_log, dn_dt_bias, dn_norm_g, need_ctx)
    y_lat = jnp.concatenate([ya_l, yb_l, yc_l], -1)
    y_ctx = None
    if need_ctx:
        ya_c = chunk_token_mlp(p_ctx[..., :s_a], gmlp_ln_g, gmlp_ln_b, gmlp_w_s, gmlp_b_s)
        y_ctx = jnp.concatenate([ya_c, yb_c, yc_c], -1)
    return y_ctx, y_lat


def deepnorm_update(x, branch, gate, ln_g, ln_b):
    return layer_norm(DEEPNORM_ALPHA * x + gate * branch, ln_g, ln_b)


def sq_relu_mlp(h, w_up, w_down):
    return jnp.square(jax.nn.relu(h @ w_up)) @ w_down


def setup_inputs(seed: int = 0) -> dict:
    key = jax.random.key(seed)
    ks = jax.random.split(key, 26)
    f32 = jnp.float32
    nrm = lambda k, shape, scale: jax.random.normal(k, shape, f32) * scale
    L, D = DEPTH, D_MODEL
    dt = jnp.exp(jax.random.uniform(ks[16], (L, 2, DN_HEADS), f32, math.log(1e-3), math.log(1e-1)))
    return {
        "x": nrm(ks[0], (BATCH, SEQ, D), 1.0),
        "c": nrm(ks[1], (BATCH, D), 1.0),
        "ctx": nrm(ks[2], (BATCH, CTX_LEN, D), 1.0),
        "c_ctx": nrm(ks[3], (D,), 1.0),
        "mod_w": nrm(ks[4], (L, D, 6 * D), D ** -0.5),
        "mod_b": nrm(ks[5], (L, 6 * D), 0.01),
        "w_in": nrm(ks[6], (L, D, IN_COLS), D ** -0.5),
        "w_out": nrm(ks[7], (L, MIX_WIDTH, D), MIX_WIDTH ** -0.5 * DEEPNORM_BETA),
        "gmlp_ln_g": 1.0 + nrm(ks[8], (L, A_WIDTH), 0.02),
        "gmlp_ln_b": nrm(ks[9], (L, A_WIDTH), 0.02),
        "gmlp_w_s": nrm(ks[10], (L, A_GROUPS, MLP_CHUNK, MLP_CHUNK), MLP_CHUNK ** -0.5),
        "gmlp_b_s": 1.0 + nrm(ks[11], (L, A_GROUPS, MLP_CHUNK), 0.02),
        "attn_q_g": 1.0 + nrm(ks[12], (L, HEAD_DIM), 0.02),
        "attn_k_g": 1.0 + nrm(ks[13], (L, HEAD_DIM), 0.02),
        "dn_conv_w": nrm(ks[14], (L, DN_CONV, 3 * DN_WIDTH), DN_CONV ** -0.5),
        "dn_a_log": jnp.log(jax.random.uniform(ks[15], (L, 2, DN_HEADS), f32, 1.0, 16.0)),
        "dn_dt_bias": dt + jnp.log(-jnp.expm1(-dt)),
        "dn_norm_g": 1.0 + nrm(ks[17], (L, HEAD_DIM), 0.02),
        "ln1_g": 1.0 + nrm(ks[18], (L, D), 0.02),
        "ln1_b": nrm(ks[19], (L, D), 0.02),
        "ln2_g": 1.0 + nrm(ks[20], (L, D), 0.02),
        "ln2_b": nrm(ks[21], (L, D), 0.02),
        "w_up": nrm(ks[22], (L, D, D_FF), D ** -0.5),
        "w_down": nrm(ks[23], (L, D_FF, D), D_FF ** -0.5 * DEEPNORM_BETA),
    }


def reference(x, c, ctx, c_ctx, mod_w, mod_b, w_in, w_out, gmlp_ln_g, gmlp_ln_b, gmlp_w_s, gmlp_b_s,
              attn_q_g, attn_k_g, dn_conv_w, dn_a_log, dn_dt_bias, dn_norm_g,
              ln1_g, ln1_b, ln2_g, ln2_b, w_up, w_down):
    n = x.shape[1]
    rows = n // GRID_W
    ang_row, ang_col = axial_rope_angles(rows)
    x_lat, x_ctx = x, ctx
    for i in range(DEPTH):
        need_ctx = i < DEPTH - 1
        mod_l = (jax.nn.silu(c) @ mod_w[i] + mod_b[i])[:, None, :]
        mod_c = (jax.nn.silu(c_ctx) @ mod_w[i] + mod_b[i])[None, None, :]
        sh1_l, sc1_l, g1_l, sh2_l, sc2_l, g2_l = jnp.split(mod_l, 6, -1)
        sh1_c, sc1_c, g1_c, sh2_c, sc2_c, g2_c = jnp.split(mod_c, 6, -1)
        p_lat = modulate(x_lat, sh1_l, sc1_l) @ w_in[i]
        p_ctx = modulate(x_ctx, sh1_c, sc1_c) @ w_in[i]
        y_ctx, y_lat = token_mixers(p_ctx, p_lat, ang_row, ang_col, gmlp_ln_g[i], gmlp_ln_b[i], gmlp_w_s[i],
                                    gmlp_b_s[i], attn_q_g[i], attn_k_g[i], dn_conv_w[i], dn_a_log[i],
                                    dn_dt_bias[i], dn_norm_g[i], need_ctx)
        x_lat = deepnorm_update(x_lat, y_lat @ w_out[i], g1_l, ln1_g[i], ln1_b[i])
        m_lat = sq_relu_mlp(modulate(x_lat, sh2_l, sc2_l), w_up[i], w_down[i])
        x_lat = deepnorm_update(x_lat, m_lat, g2_l, ln2_g[i], ln2_b[i])
        if need_ctx:
            x_ctx = deepnorm_update(x_ctx, y_ctx @ w_out[i], g1_c, ln1_g[i], ln1_b[i])
            m_ctx = sq_relu_mlp(modulate(x_ctx, sh2_c, sc2_c), w_up[i], w_down[i])
            x_ctx = deepnorm_update(x_ctx, m_ctx, g2_c, ln2_g[i], ln2_b[i])
    return x_lat
```

```python
import functools
import math

import jax
import jax.numpy as jnp
from jax import lax
from jax.experimental import pallas as pl
from jax.experimental.pallas import tpu as pltpu

F32 = jnp.float32
BF16 = jnp.bfloat16

GRID_W = 64
HEAD_DIM = 64
A_GROUPS = 4
A_WIDTH = A_GROUPS * HEAD_DIM
MLP_CHUNK = 128
ATT_Q_HEADS = 6
ATT_KV_HEADS = 2
ATT_GROUP = ATT_Q_HEADS // ATT_KV_HEADS
ATT_WIDTH = ATT_Q_HEADS * HEAD_DIM
KV_WIDTH = ATT_KV_HEADS * HEAD_DIM
ROPE_THETA = 10000.0
DN_HEADS = 6
DN_WIDTH = DN_HEADS * HEAD_DIM
DN_CONV = 5
DN_CHUNK = 64
A_COLS = 2 * A_WIDTH
B_COLS = ATT_WIDTH + 2 * KV_WIDTH
QKV_COLS = 3 * DN_WIDTH
GATE_COLS = 4 * DN_HEADS
IN_COLS = A_COLS + B_COLS + QKV_COLS + DN_WIDTH + GATE_COLS
LANES = 128
SUBLANES = 8
IN_COLS_PAD = IN_COLS - GATE_COLS + LANES
EPS = 1e-6
MOD_ROWS = 8
VMEM_LIMIT = 56 * 1024 * 1024


def _dot(a, b):
    return jnp.dot(a, b, preferred_element_type=F32)


def _dot_nt(a, b):
    return lax.dot_general(a, b, (((1,), (1,)), ((), ())), preferred_element_type=F32)


def _dot_tn(a, b):
    return lax.dot_general(a, b, (((0,), (0,)), ((), ())), preferred_element_type=F32)


def _split(x):
    hi = x.astype(BF16)
    lo = (x - hi.astype(F32)).astype(BF16)
    return hi, lo


def _dot_x3(a, b):
    ah, al = _split(a)
    bh, bl = _split(b)
    return _dot(ah, bh) + (_dot(ah, bl) + _dot(al, bh))


def _dot_lhs_split(x, m):
    hi, lo = _split(x)
    return _dot(hi, m) + _dot(lo, m)


def _norm(x):
    mu = jnp.mean(x, -1, keepdims=True)
    xc = x - mu
    var = jnp.mean(xc * xc, -1, keepdims=True)
    return xc * lax.rsqrt(var + EPS)


def _sigmoid(x):
    return 1.0 / (1.0 + jnp.exp(-x))


def _silu(x):
    return x * _sigmoid(x)


def _params(n_grid, vmem=VMEM_LIMIT):
    return pltpu.CompilerParams(dimension_semantics=("arbitrary",) * n_grid, vmem_limit_bytes=vmem)


def _const_spec(shape):
    nd = len(shape)
    return pl.BlockSpec(shape, lambda *_: (0,) * nd)


def _mod_kernel(cs_ref, w_ref, b_ref, o_ref):
    cs = cs_ref[...]
    o_ref[0] = _dot_x3(_silu(cs), w_ref[0]) + b_ref[0]


def _modulation(cs, mod_w, mod_b):
    depth, d, n = mod_w.shape
    tn = 1536
    return pl.pallas_call(
        _mod_kernel,
        grid=(depth, n // tn),
        in_specs=[
            pl.BlockSpec((MOD_ROWS, d), lambda l, j: (0, 0)),
            pl.BlockSpec((1, d, tn), lambda l, j: (l, 0, j)),
            pl.BlockSpec((1, 1, tn), lambda l, j: (l, 0, j)),
        ],
        out_specs=pl.BlockSpec((1, MOD_ROWS, tn), lambda l, j: (l, 0, j)),
        out_shape=jax.ShapeDtypeStruct((depth, MOD_ROWS, n), F32),
        compiler_params=_params(2),
        name="modulation",
    )(cs, mod_w, mod_b.reshape(depth, 1, n))


def _pre_kernel(x_ref, sh_ref, sc_ref, w_ref, pa_ref, pb_ref, pq_ref, pz_ref, pg_ref):
    h = _norm(x_ref[0]) * (1.0 + sc_ref[...]) + sh_ref[...]
    p = _dot(h.astype(BF16), w_ref[...])
    o = 0
    for ref in (pa_ref, pb_ref, pq_ref, pz_ref, pg_ref):
        w = ref.shape[-1]
        ref[0] = p[:, o:o + w]
        o += w


def _input_projection(xcat, modl, w_in_p, tm, ncb, ctx_row):
    bsz, t, d = xcat.shape
    widths = (A_COLS, B_COLS, QKV_COLS, DN_WIDTH, LANES)
    mspec = lambda c: pl.BlockSpec((None, None, 1, d), lambda b, i: (jnp.where(i < ncb, ctx_row, b), c, 0, 0))
    return pl.pallas_call(
        _pre_kernel,
        grid=(bsz, t // tm),
        in_specs=[
            pl.BlockSpec((1, tm, d), lambda b, i: (b, i, 0)),
            mspec(0), mspec(1),
            _const_spec(w_in_p.shape),
        ],
        out_specs=[pl.BlockSpec((1, tm, w), lambda b, i: (b, i, 0)) for w in widths],
        out_shape=[jax.ShapeDtypeStruct((bsz, t, w), F32) for w in widths],
        compiler_params=_params(2),
        name="input_projection",
    )(xcat, modl, modl, w_in_p)


def _gmlp_kernel(pa_ref, lng_ref, lnb_ref, ws_ref, bias_ref, o_ref, *, nchunk):
    for ci in range(nchunk):
        rows = slice(ci * MLP_CHUNK, (ci + 1) * MLP_CHUNK)
        p = pa_ref[0, rows, :]
        a = 0.5 * p * (1.0 + lax.erf(p * (2.0 ** -0.5)))
        u = a[:, :A_WIDTH]
        v = _norm(a[:, A_WIDTH:]) * lng_ref[...] + lnb_ref[...]
        vb = v.astype(BF16)
        mixed = jnp.concatenate(
            [_dot(ws_ref[g], vb[:, g * HEAD_DIM:(g + 1) * HEAD_DIM]) for g in range(A_GROUPS)], axis=-1)
        o_ref[0, rows, :] = (u * (mixed + bias_ref[...])).astype(BF16)


def _gmlp(pa, ln_g, ln_b, w_s, b_s, tm):
    bsz, t, _ = pa.shape
    bias = jnp.repeat(b_s.T, HEAD_DIM, axis=1)
    return pl.pallas_call(
        functools.partial(_gmlp_kernel, nchunk=tm // MLP_CHUNK),
        grid=(bsz, t // tm),
        in_specs=[
            pl.BlockSpec((1, tm, A_COLS), lambda b, i: (b, i, 0)),
            _const_spec((1, A_WIDTH)), _const_spec((1, A_WIDTH)),
            _const_spec(w_s.shape), _const_spec(bias.shape),
        ],
        out_specs=pl.BlockSpec((1, tm, A_WIDTH), lambda b, i: (b, i, 0)),
        out_shape=jax.ShapeDtypeStruct((bsz, t, A_WIDTH), BF16),
        compiler_params=_params(2),
        name="gmlp",
    )(pa, ln_g.reshape(1, -1), ln_b.reshape(1, -1), w_s.astype(BF16), bias)


def _swap_rope_pairs(x):
    n = x.shape[-1]
    lane = lax.broadcasted_iota(jnp.int32, x.shape, x.ndim - 1)
    first = (lane & 31) < 16
    return jnp.where(first, pltpu.roll(x, n - 16, x.ndim - 1), pltpu.roll(x, 16, x.ndim - 1))


def _attn_prep_kernel(pb_ref, cos_ref, sin_ref, gain_ref, bavg_ref, q_out, k_out, v_out):
    pb = pb_ref[0]
    nqk = ATT_WIDTH + KV_WIDTH
    qk = pb[:, :nqk]
    ms = _dot_lhs_split(qk * qk, bavg_ref[...])
    qk = qk * lax.rsqrt(ms + EPS) * gain_ref[...]
    r = qk * cos_ref[...] + _swap_rope_pairs(qk) * sin_ref[...]
    for h in range(ATT_Q_HEADS):
        q_out[0, h] = r[:, h * HEAD_DIM:(h + 1) * HEAD_DIM].astype(BF16)
    for h in range(ATT_KV_HEADS):
        k_out[0, h] = r[:, ATT_WIDTH + h * HEAD_DIM:ATT_WIDTH + (h + 1) * HEAD_DIM].astype(BF16)
        v_out[0, h] = pb[:, nqk + h * HEAD_DIM:nqk + (h + 1) * HEAD_DIM].astype(BF16)


def _block_diag_ones(n, blk, value=1.0):
    idx = jnp.arange(n) // blk
    return jnp.where(idx[:, None] == idx[None, :], value, 0.0).astype(BF16)


def _attn_prep(pb, cos_t, sin_t, q_g, k_g, tm):
    bsz, t, _ = pb.shape
    nqk = ATT_WIDTH + KV_WIDTH
    gain = jnp.concatenate([jnp.tile(q_g, ATT_Q_HEADS), jnp.tile(k_g, ATT_KV_HEADS)]).reshape(1, nqk)
    bavg = _block_diag_ones(nqk, HEAD_DIM, 1.0 / HEAD_DIM)
    hspec = lambda nh: pl.BlockSpec((1, nh, tm, HEAD_DIM), lambda b, i: (b, 0, i, 0))
    return pl.pallas_call(
        _attn_prep_kernel,
        grid=(bsz, t // tm),
        in_specs=[
            pl.BlockSpec((1, tm, B_COLS), lambda b, i: (b, i, 0)),
            pl.BlockSpec((tm, nqk), lambda b, i: (i, 0)),
            pl.BlockSpec((tm, nqk), lambda b, i: (i, 0)),
            _const_spec((1, nqk)), _const_spec((nqk, nqk)),
        ],
        out_specs=[hspec(ATT_Q_HEADS), hspec(ATT_KV_HEADS), hspec(ATT_KV_HEADS)],
        out_shape=[jax.ShapeDtypeStruct((bsz, nh, t, HEAD_DIM), BF16)
                   for nh in (ATT_Q_HEADS, ATT_KV_HEADS, ATT_KV_HEADS)],
        compiler_params=_params(2),
        name="attn_prep",
    )(pb, cos_t, sin_t, gain, bavg)


def _attn_kernel(q_ref, k_ref, v_ref, o_ref, *, tq, tc, q_first):
    qi = pl.program_id(2) + q_first
    q = q_ref[0].reshape(ATT_GROUP * tq, HEAD_DIM)

    def attend(k, v):
        s = _dot_nt(q, k)
        p = jnp.exp(s - jnp.max(s, -1, keepdims=True))
        o = _dot(p.astype(BF16), v) / jnp.sum(p, -1, keepdims=True)
        for g in range(ATT_GROUP):
            o_ref[0, 0, :, g * HEAD_DIM:(g + 1) * HEAD_DIM] = o[g * tq:(g + 1) * tq].astype(BF16)

    @pl.when(qi < tc // tq)
    def _():
        attend(k_ref[0, 0, :tc, :], v_ref[0, 0, :tc, :])

    @pl.when(qi >= tc // tq)
    def _():
        attend(k_ref[0, 0], v_ref[0, 0])


def _attention(q, k, v, tc, need_ctx, tq=128):
    bsz, _, t, _ = q.shape
    q_first = 0 if need_ctx else tc // tq
    nq = t // tq - q_first
    gw = ATT_GROUP * HEAD_DIM
    return pl.pallas_call(
        functools.partial(_attn_kernel, tq=tq, tc=tc, q_first=q_first),
        grid=(bsz, ATT_KV_HEADS, nq),
        in_specs=[
            pl.BlockSpec((1, ATT_GROUP, tq, HEAD_DIM), lambda b, j, i: (b, j, i + q_first, 0)),
            pl.BlockSpec((1, 1, t, HEAD_DIM), lambda b, j, i: (b, j, 0, 0)),
            pl.BlockSpec((1, 1, t, HEAD_DIM), lambda b, j, i: (b, j, 0, 0)),
        ],
        out_specs=pl.BlockSpec((1, 1, tq, gw), lambda b, j, i: (b, j, i + q_first, 0)),
        out_shape=jax.ShapeDtypeStruct((bsz, ATT_KV_HEADS, t, gw), BF16),
        compiler_params=_params(3),
        name="attention",
    )(q, k, v)


HALO = SUBLANES


def _dn_prep_kernel(x_ref, xp_ref, xn_ref, pg_ref, cw_ref, bsum_ref, al_ref, dt_ref,
                    q_out, k_out, v_out, g_out, xe_ref, *, tm, ncb, nblk):
    i = pl.program_id(1)
    seg_start = jnp.logical_or(i == 0, i == ncb)
    seg_end = jnp.logical_or(i == ncb - 1, i == nblk - 1)
    xe_ref[0:HALO, :] = jnp.where(seg_start, 0.0, xp_ref[0])
    xe_ref[HALO:HALO + tm, :] = x_ref[0]
    xe_ref[HALO + tm:2 * HALO + tm, :] = jnp.where(seg_end, 0.0, xn_ref[0])
    first = HALO - DN_CONV // 2
    acc = cw_ref[0:1, :] * xe_ref[first:first + tm, :]
    for j in range(1, DN_CONV):
        acc = acc + cw_ref[j:j + 1, :] * xe_ref[first + j:first + j + tm, :]
    y = _silu(acc)
    qk = y[:, :2 * DN_WIDTH]
    ss = _dot_lhs_split(qk * qk, bsum_ref[...])
    qk = qk * lax.rsqrt(ss + EPS)
    q_out[0] = qk[:, :DN_WIDTH] * (HEAD_DIM ** -0.5)
    k_out[0] = qk[:, DN_WIDTH:]
    v_out[0] = y[:, 2 * DN_WIDTH:]
    pg = pg_ref[0]
    z = pg + dt_ref[...]
    softplus = jnp.maximum(z, 0.0) + jnp.log1p(jnp.exp(-jnp.abs(z)))
    lane = lax.broadcasted_iota(jnp.int32, pg.shape, 1)
    g_out[0] = jnp.where(lane < 2 * DN_HEADS, -jnp.exp(al_ref[...]) * softplus, _sigmoid(pg))


def _dn_prep(pq, pg, conv_w, a_log, dt_bias, tm, ncb):
    bsz, t, c = pq.shape
    nblk = t // tm
    hb = tm // HALO
    pad = lambda v: jnp.zeros((1, LANES), F32).at[0, :2 * DN_HEADS].set(v.reshape(-1))
    bsum = _block_diag_ones(2 * DN_WIDTH, HEAD_DIM)
    rspec = lambda w: pl.BlockSpec((1, tm, w), lambda b, i: (b, i, 0))
    return pl.pallas_call(
        functools.partial(_dn_prep_kernel, tm=tm, ncb=ncb, nblk=nblk),
        grid=(bsz, nblk),
        in_specs=[
            rspec(c),
            pl.BlockSpec((1, HALO, c), lambda b, i: (b, jnp.maximum(i * hb - 1, 0), 0)),
            pl.BlockSpec((1, HALO, c), lambda b, i: (b, jnp.minimum((i + 1) * hb, nblk * hb - 1), 0)),
            rspec(LANES),
            _const_spec(conv_w.shape), _const_spec(bsum.shape),
            _const_spec((1, LANES)), _const_spec((1, LANES)),
        ],
        out_specs=[rspec(DN_WIDTH), rspec(DN_WIDTH), rspec(DN_WIDTH), rspec(LANES)],
        out_shape=[jax.ShapeDtypeStruct((bsz, t, w), F32) for w in (DN_WIDTH, DN_WIDTH, DN_WIDTH, LANES)],
        scratch_shapes=[pltpu.VMEM((tm + 2 * HALO, c), F32)],
        compiler_params=_params(2),
        name="dn_prep",
    )(pq, pq, pq, pg, conv_w, bsum, pad(a_log), pad(dt_bias))


def _unit_tri_inverse(low, ri, ci):
    n = low.shape[0]
    eye = (ri == ci).astype(F32)

    def bd(blk):
        sh = int(math.log2(blk))
        return jnp.right_shift(ri, sh) == jnp.right_shift(ci, sh)

    base = 8
    nb = -jnp.where(bd(base), low, 0.0)
    x = eye + nb
    pw = nb
    for _ in range(int(math.log2(base)) - 1):
        pw = _dot_x3(pw, pw)
        x = x + _dot_x3(x, pw)
    blk = base
    while blk < n:
        c = jnp.where(jnp.logical_and(bd(2 * blk), jnp.logical_not(bd(blk))), low, 0.0)
        x = x - _dot_x3(_dot_x3(x, c), x)
        blk *= 2
    return x


def _dn_scan_kernel(qf, kf, vf, gf, qb, kb_, vb, gb, of, ob, s_ref):
    c = DN_CHUNK

    @pl.when(pl.program_id(1) == 0)
    def _():
        s_ref[...] = jnp.zeros_like(s_ref)

    ri = lax.broadcasted_iota(jnp.int32, (c, c), 0)
    ci = lax.broadcasted_iota(jnp.int32, (c, c), 1)
    for d, (q_ref, k_ref, v_ref, g_ref, o_ref) in enumerate(((qf, kf, vf, gf, of), (qb, kb_, vb, gb, ob))):
        incl = (ci <= ri) if d == 0 else (ci >= ri)
        strict = (ci < ri) if d == 0 else (ci > ri)
        gates = g_ref[0]
        gcum = _dot_x3(incl.astype(F32), gates)
        gcum_t = gcum.T
        gtot = jnp.sum(gates, 0, keepdims=True)
        outs = []
        for h in range(DN_HEADS):
            hs = slice(h * HEAD_DIM, (h + 1) * HEAD_DIM)
            gi = d * DN_HEADS + h
            bi = 2 * DN_HEADS + gi
            q = q_ref[0, :, hs]
            k = k_ref[0, :, hs]
            v = v_ref[0, :, hs]
            beta = gates[:, bi:bi + 1]
            gcol = gcum[:, gi:gi + 1]
            grow = gcum_t[gi:gi + 1, :]
            glast = gtot[:, gi:gi + 1]
            decay = jnp.where(incl, jnp.exp(jnp.minimum(gcol - grow, 0.0)), 0.0)
            kbeta = k * beta
            k16 = k.astype(BF16)
            low = jnp.where(strict, _dot_nt(kbeta.astype(BF16), k16) * decay, 0.0)
            tinv = _unit_tri_inverse(low, ri, ci)
            eg = jnp.exp(gcol)
            rhs = jnp.concatenate([v * beta, kbeta * eg], axis=-1)
            sol = _dot_x3(tinv, rhs)
            u = sol[:, :HEAD_DIM]
            w = sol[:, HEAD_DIM:]
            intra = jnp.where(incl, _dot_nt(q.astype(BF16), k16) * decay, 0.0)
            q_dec = q * eg
            k_dec = k * jnp.exp(glast - gcol)
            s = s_ref[d, h]
            s16 = s.astype(BF16)
            v_new = u - _dot(w.astype(BF16), s16)
            vn16 = v_new.astype(BF16)
            outs.append(_dot(q_dec.astype(BF16), s16) + _dot(intra.astype(BF16), vn16))
            s_ref[d, h] = s * jnp.exp(glast) + _dot_tn(k_dec.astype(BF16), vn16)
        o_ref[0] = jnp.concatenate(outs, axis=-1)


def _dn_scan(q, k, v, g, tc):
    bsz, t, w = q.shape
    c = DN_CHUNK
    nct, ncx = t // c, tc // c

    def bwd_blk(s):
        return jnp.where(s < ncx, ncx - 1 - s, nct - 1 - s + ncx)

    fspec = lambda width: pl.BlockSpec((1, c, width), lambda b, s: (b, s, 0))
    bspec = lambda width: pl.BlockSpec((1, c, width), lambda b, s: (b, bwd_blk(s), 0))
    return pl.pallas_call(
        _dn_scan_kernel,
        grid=(bsz, nct),
        in_specs=[fspec(w), fspec(w), fspec(w), fspec(LANES), bspec(w), bspec(w), bspec(w), bspec(LANES)],
        out_specs=[fspec(w), bspec(w)],
        out_shape=[jax.ShapeDtypeStruct((bsz, t, w), F32)] * 2,
        scratch_shapes=[pltpu.VMEM((2, DN_HEADS, HEAD_DIM, HEAD_DIM), F32)],
        compiler_params=_params(2),
        name="dn_scan",
    )(q, k, v, g, q, k, v, g)


def _dn_gate_kernel(of_ref, ob_ref, z_ref, gain_ref, bavg_ref, y_ref):
    o = of_ref[0] + ob_ref[0]
    ms = _dot_lhs_split(o * o, bavg_ref[...])
    y_ref[0] = (o * lax.rsqrt(ms + EPS) * gain_ref[...] * _silu(z_ref[0])).astype(BF16)


def _dn_gate(o_f, o_b, pz, norm_g, tm):
    bsz, t, w = o_f.shape
    gain = jnp.tile(norm_g, DN_HEADS).reshape(1, w)
    bavg = _block_diag_ones(w, HEAD_DIM, 1.0 / HEAD_DIM)
    return pl.pallas_call(
        _dn_gate_kernel,
        grid=(bsz, t // tm),
        in_specs=[
            pl.BlockSpec((1, tm, w), lambda b, i: (b, i, 0)),
            pl.BlockSpec((1, tm, w), lambda b, i: (b, i, 0)),
            pl.BlockSpec((1, tm, w), lambda b, i: (b, i, 0)),
            _const_spec((1, w)), _const_spec((w, w)),
        ],
        out_specs=pl.BlockSpec((1, tm, w), lambda b, i: (b, i, 0)),
        out_shape=jax.ShapeDtypeStruct((bsz, t, w), BF16),
        compiler_params=_params(2),
        name="dn_gate",
    )(o_f, o_b, pz, gain, bavg)


def _post_kernel(x_ref, ya_ref, yb0_ref, yb1_ref, yc_ref, g1_ref, sh2_ref, sc2_ref, g2_ref,
                 wa_ref, wb0_ref, wb1_ref, wc_ref, l1g_ref, l1b_ref, l2g_ref, l2b_ref,
                 wup_ref, wdn_ref, o_ref, *, alpha, f_chunk):
    branch = (_dot(ya_ref[0], wa_ref[...]) + _dot(yb0_ref[0, 0], wb0_ref[...])
              + _dot(yb1_ref[0, 0], wb1_ref[...]) + _dot(yc_ref[0], wc_ref[...]))
    x1 = _norm(alpha * x_ref[0] + g1_ref[...] * branch) * l1g_ref[...] + l1b_ref[...]
    h = (_norm(x1) * (1.0 + sc2_ref[...]) + sh2_ref[...]).astype(BF16)
    d_ff = wup_ref.shape[1]
    m = None
    for f in range(0, d_ff, f_chunk):
        up = jnp.maximum(_dot(h, wup_ref[:, f:f + f_chunk]), 0.0)
        part = _dot((up * up).astype(BF16), wdn_ref[f:f + f_chunk, :])
        m = part if m is None else m + part
    o_ref[0] = _norm(alpha * x1 + g2_ref[...] * m) * l2g_ref[...] + l2b_ref[...]


def _post(xcat, ya, yb, yc, modl, w_out, ln1_g, ln1_b, ln2_g, ln2_b, w_up, w_down, tm, ncb, ctx_row, alpha,
          need_ctx):
    bsz, t, d = xcat.shape
    first = 0 if need_ctx else ncb
    gw = ATT_GROUP * HEAD_DIM
    o1 = A_WIDTH
    wa, wb0, wb1, wc = w_out[:o1], w_out[o1:o1 + gw], w_out[o1 + gw:o1 + 2 * gw], w_out[o1 + 2 * gw:]
    mspec = lambda c: pl.BlockSpec(
        (None, None, 1, d), lambda b, i: (jnp.where(i + first < ncb, ctx_row, b), c, 0, 0))
    rspec = lambda w: pl.BlockSpec((1, tm, w), lambda b, i: (b, i + first, 0))
    vec = lambda a: a.reshape(1, d)
    wspec = lambda a: pl.BlockSpec(a.shape, lambda b, i: (0, 0), pipeline_mode=pl.Buffered(1))
    weights = [w.astype(BF16) for w in (wa, wb0, wb1, wc)]
    w_up, w_down = w_up.astype(BF16), w_down.astype(BF16)
    return pl.pallas_call(
        functools.partial(_post_kernel, alpha=alpha, f_chunk=1024),
        grid=(bsz, t // tm - first),
        in_specs=[
            rspec(d), rspec(A_WIDTH),
            pl.BlockSpec((1, 1, tm, gw), lambda b, i: (b, 0, i + first, 0)),
            pl.BlockSpec((1, 1, tm, gw), lambda b, i: (b, 1, i + first, 0)),
            rspec(DN_WIDTH),
            mspec(2), mspec(3), mspec(4), mspec(5),
            *[wspec(w) for w in weights],
            _const_spec((1, d)), _const_spec((1, d)), _const_spec((1, d)), _const_spec((1, d)),
            wspec(w_up), wspec(w_down),
        ],
        out_specs=pl.BlockSpec((1, tm, d), lambda b, i: (b, i, 0)),
        out_shape=jax.ShapeDtypeStruct((bsz, t - first * tm, d), F32),
        compiler_params=_params(2),
        name="post_mixer",
    )(xcat, ya, yb, yb, yc, modl, modl, modl, modl, *weights,
      vec(ln1_g), vec(ln1_b), vec(ln2_g), vec(ln2_b), w_up, w_down)


def _rope_tables(tc, tl):
    pos = jnp.arange(tl)
    row = (pos // GRID_W).astype(F32)
    col = (pos % GRID_W).astype(F32)
    half = HEAD_DIM // 2
    inv = 1.0 / (ROPE_THETA ** (jnp.arange(0, half, 2, dtype=F32) / half))
    ar, ac = row[:, None] * inv, col[:, None] * inv
    cos = jnp.concatenate([jnp.cos(ar), jnp.cos(ar), jnp.cos(ac), jnp.cos(ac)], -1)
    sin = jnp.concatenate([-jnp.sin(ar), jnp.sin(ar), -jnp.sin(ac), jnp.sin(ac)], -1)
    cos = jnp.concatenate([jnp.ones((tc, HEAD_DIM), F32), cos], 0)
    sin = jnp.concatenate([jnp.zeros((tc, HEAD_DIM), F32), sin], 0)
    scale = HEAD_DIM ** -0.5
    tile = lambda a: jnp.concatenate([jnp.tile(a, (1, ATT_Q_HEADS)) * scale, jnp.tile(a, (1, ATT_KV_HEADS))], -1)
    return tile(cos), tile(sin)


def kernel(x, c, ctx, c_ctx, mod_w, mod_b, w_in, w_out, gmlp_ln_g, gmlp_ln_b, gmlp_w_s, gmlp_b_s,
           attn_q_g, attn_k_g, dn_conv_w, dn_a_log, dn_dt_bias, dn_norm_g,
           ln1_g, ln1_b, ln2_g, ln2_b, w_up, w_down):
    bsz, tl, d = x.shape
    tc = ctx.shape[1]
    depth = mod_w.shape[0]
    assert bsz < MOD_ROWS and tl % GRID_W == 0 and tc % MLP_CHUNK == 0 and tl % MLP_CHUNK == 0
    tm = 256 if tc % 256 == 0 else MLP_CHUNK
    ncb = tc // tm
    ctx_row = bsz
    alpha = (2 * depth) ** 0.25

    cs = jnp.zeros((MOD_ROWS, d), F32).at[:bsz].set(c).at[ctx_row].set(c_ctx)
    mod = _modulation(cs, mod_w, mod_b).reshape(depth, MOD_ROWS, 6, 1, d)
    cos_t, sin_t = _rope_tables(tc, tl)
    xcat = jnp.concatenate([ctx, x], axis=1)

    for l in range(depth):
        need_ctx = l < depth - 1
        w_in_p = jnp.pad(w_in[l], ((0, 0), (0, IN_COLS_PAD - IN_COLS))).astype(BF16)
        pa, pb, pq, pz, pg = _input_projection(xcat, mod[l], w_in_p, tm, ncb, ctx_row)
        ya = _gmlp(pa, gmlp_ln_g[l], gmlp_ln_b[l], gmlp_w_s[l], gmlp_b_s[l], tm)
        qa, ka, va = _attn_prep(pb, cos_t, sin_t, attn_q_g[l], attn_k_g[l], tm)
        yb = _attention(qa, ka, va, tc, need_ctx)
        dq, dk, dv, dg = _dn_prep(pq, pg, dn_conv_w[l], dn_a_log[l], dn_dt_bias[l], tm, ncb)
        o_f, o_b = _dn_scan(dq, dk, dv, dg, tc)
        yc = _dn_gate(o_f, o_b, pz, dn_norm_g[l], tm)
        xcat = _post(xcat, ya, yb, yc, mod[l], w_out[l], ln1_g[l], ln1_b[l], ln2_g[l], ln2_b[l],
                     w_up[l], w_down[l], tm, ncb, ctx_row, alpha, need_ctx)
    return xcat
```

```python
import functools
import math

import jax
import jax.numpy as jnp
from jax import lax
from jax.experimental import pallas as pl
from jax.experimental.pallas import tpu as pltpu

F32 = jnp.float32
BF16 = jnp.bfloat16

GRID_W = 64
HEAD_DIM = 64
A_GROUPS = 4
A_WIDTH = A_GROUPS * HEAD_DIM
MLP_CHUNK = 128
ATT_Q_HEADS = 6
ATT_KV_HEADS = 2
ATT_GROUP = ATT_Q_HEADS // ATT_KV_HEADS
ATT_WIDTH = ATT_Q_HEADS * HEAD_DIM
KV_WIDTH = ATT_KV_HEADS * HEAD_DIM
ROPE_THETA = 10000.0
DN_HEADS = 6
DN_WIDTH = DN_HEADS * HEAD_DIM
DN_CONV = 5
DN_CHUNK = 64
A_COLS = 2 * A_WIDTH
B_COLS = ATT_WIDTH + 2 * KV_WIDTH
QKV_COLS = 3 * DN_WIDTH
GATE_COLS = 4 * DN_HEADS
IN_COLS = A_COLS + B_COLS + QKV_COLS + DN_WIDTH + GATE_COLS
LANES = 128
SUBLANES = 8
IN_COLS_PAD = IN_COLS - GATE_COLS + LANES
EPS = 1e-6
MOD_ROWS = 8
VMEM_LIMIT = 56 * 1024 * 1024


def _dot(a, b):
    return jnp.dot(a, b, preferred_element_type=F32)


def _dot_nt(a, b):
    return lax.dot_general(a, b, (((1,), (1,)), ((), ())), preferred_element_type=F32)


def _dot_tn(a, b):
    return lax.dot_general(a, b, (((0,), (0,)), ((), ())), preferred_element_type=F32)


def _split(x):
    hi = x.astype(BF16)
    lo = (x - hi.astype(F32)).astype(BF16)
    return hi, lo


def _dot_x3(a, b):
    ah, al = _split(a)
    bh, bl = _split(b)
    return _dot(ah, bh) + (_dot(ah, bl) + _dot(al, bh))


def _dot_lhs_split(x, m):
    hi, lo = _split(x)
    return _dot(hi, m) + _dot(lo, m)


def _norm(x):
    mu = jnp.mean(x, -1, keepdims=True)
    xc = x - mu
    var = jnp.mean(xc * xc, -1, keepdims=True)
    return xc * lax.rsqrt(var + EPS)


def _sigmoid(x):
    return 1.0 / (1.0 + jnp.exp(-x))


def _silu(x):
    return x * _sigmoid(x)


def _params(n_grid, vmem=VMEM_LIMIT):
    return pltpu.CompilerParams(dimension_semantics=("arbitrary",) * n_grid, vmem_limit_bytes=vmem)


def _const_spec(shape):
    nd = len(shape)
    return pl.BlockSpec(shape, lambda *_: (0,) * nd)


def _mod_kernel(cs_ref, w_ref, b_ref, o_ref):
    cs = cs_ref[...]
    o_ref[0] = _dot_x3(_silu(cs), w_ref[0]) + b_ref[0]


def _modulation(cs, mod_w, mod_b):
    depth, d, n = mod_w.shape
    tn = 1536
    return pl.pallas_call(
        _mod_kernel,
        grid=(depth, n // tn),
        in_specs=[
            pl.BlockSpec((MOD_ROWS, d), lambda l, j: (0, 0)),
            pl.BlockSpec((1, d, tn), lambda l, j: (l, 0, j)),
            pl.BlockSpec((1, 1, tn), lambda l, j: (l, 0, j)),
        ],
        out_specs=pl.BlockSpec((1, MOD_ROWS, tn), lambda l, j: (l, 0, j)),
        out_shape=jax.ShapeDtypeStruct((depth, MOD_ROWS, n), F32),
        compiler_params=_params(2),
        name="modulation",
    )(cs, mod_w, mod_b.reshape(depth, 1, n))


def _pre_kernel(x_ref, sh_ref, sc_ref, w_ref, pa_ref, pb_ref, pq_ref, pz_ref, pg_ref):
    h = _norm(x_ref[0]) * (1.0 + sc_ref[...]) + sh_ref[...]
    p = _dot(h.astype(BF16), w_ref[...])
    o = 0
    for ref in (pa_ref, pb_ref, pq_ref, pz_ref, pg_ref):
        w = ref.shape[-1]
        ref[0] = p[:, o:o + w]
        o += w


def _input_projection(xcat, modl, w_in_p, tm, ncb, ctx_row):
    bsz, t, d = xcat.shape
    widths = (A_COLS, B_COLS, QKV_COLS, DN_WIDTH, LANES)
    mspec = lambda c: pl.BlockSpec((None, None, 1, d), lambda b, i: (jnp.where(i < ncb, ctx_row, b), c, 0, 0))
    return pl.pallas_call(
        _pre_kernel,
        grid=(bsz, t // tm),
        in_specs=[
            pl.BlockSpec((1, tm, d), lambda b, i: (b, i, 0)),
            mspec(0), mspec(1),
            _const_spec(w_in_p.shape),
        ],
        out_specs=[pl.BlockSpec((1, tm, w), lambda b, i: (b, i, 0)) for w in widths],
        out_shape=[jax.ShapeDtypeStruct((bsz, t, w), F32) for w in widths],
        compiler_params=_params(2),
        name="input_projection",
    )(xcat, modl, modl, w_in_p)


def _gmlp_kernel(pa_ref, lng_ref, lnb_ref, ws_ref, bias_ref, o_ref, *, nchunk):
    for ci in range(nchunk):
        rows = slice(ci * MLP_CHUNK, (ci + 1) * MLP_CHUNK)
        p = pa_ref[0, rows, :]
        a = 0.5 * p * (1.0 + lax.erf(p * (2.0 ** -0.5)))
        u = a[:, :A_WIDTH]
        v = _norm(a[:, A_WIDTH:]) * lng_ref[...] + lnb_ref[...]
        vb = v.astype(BF16)
        mixed = jnp.concatenate(
            [_dot(ws_ref[g], vb[:, g * HEAD_DIM:(g + 1) * HEAD_DIM]) for g in range(A_GROUPS)], axis=-1)
        o_ref[0, rows, :] = (u * (mixed + bias_ref[...])).astype(BF16)


def _gmlp(pa, ln_g, ln_b, w_s, b_s, tm):
    bsz, t, _ = pa.shape
    bias = jnp.repeat(b_s.T, HEAD_DIM, axis=1)
    return pl.pallas_call(
        functools.partial(_gmlp_kernel, nchunk=tm // MLP_CHUNK),
        grid=(bsz, t // tm),
        in_specs=[
            pl.BlockSpec((1, tm, A_COLS), lambda b, i: (b, i, 0)),
            _const_spec((1, A_WIDTH)), _const_spec((1, A_WIDTH)),
            _const_spec(w_s.shape), _const_spec(bias.shape),
        ],
        out_specs=pl.BlockSpec((1, tm, A_WIDTH), lambda b, i: (b, i, 0)),
        out_shape=jax.ShapeDtypeStruct((bsz, t, A_WIDTH), BF16),
        compiler_params=_params(2),
        name="gmlp",
    )(pa, ln_g.reshape(1, -1), ln_b.reshape(1, -1), w_s.astype(BF16), bias)


def _swap_rope_pairs(x):
    n = x.shape[-1]
    lane = lax.broadcasted_iota(jnp.int32, x.shape, x.ndim - 1)
    first = (lane & 31) < 16
    return jnp.where(first, pltpu.roll(x, n - 16, x.ndim - 1), pltpu.roll(x, 16, x.ndim - 1))


def _attn_prep_kernel(pb_ref, cos_ref, sin_ref, gain_ref, bavg_ref, q_out, k_out, v_out):
    pb = pb_ref[0]
    nqk = ATT_WIDTH + KV_WIDTH
    qk = pb[:, :nqk]
    ms = _dot_lhs_split(qk * qk, bavg_ref[...])
    qk = qk * lax.rsqrt(ms + EPS) * gain_ref[...]
    r = qk * cos_ref[...] + _swap_rope_pairs(qk) * sin_ref[...]
    for h in range(ATT_Q_HEADS):
        q_out[0, h] = r[:, h * HEAD_DIM:(h + 1) * HEAD_DIM].astype(BF16)
    for h in range(ATT_KV_HEADS):
        k_out[0, h] = r[:, ATT_WIDTH + h * HEAD_DIM:ATT_WIDTH + (h + 1) * HEAD_DIM].astype(BF16)
        v_out[0, h] = pb[:, nqk + h * HEAD_DIM:nqk + (h + 1) * HEAD_DIM].astype(BF16)


def _block_diag_ones(n, blk, value=1.0):
    idx = jnp.arange(n) // blk
    return jnp.where(idx[:, None] == idx[None, :], value, 0.0).astype(BF16)


def _attn_prep(pb, cos_t, sin_t, q_g, k_g, tm):
    bsz, t, _ = pb.shape
    nqk = ATT_WIDTH + KV_WIDTH
    gain = jnp.concatenate([jnp.tile(q_g, ATT_Q_HEADS), jnp.tile(k_g, ATT_KV_HEADS)]).reshape(1, nqk)
    bavg = _block_diag_ones(nqk, HEAD_DIM, 1.0 / HEAD_DIM)
    hspec = lambda nh: pl.BlockSpec((1, nh, tm, HEAD_DIM), lambda b, i: (b, 0, i, 0))
    return pl.pallas_call(
        _attn_prep_kernel,
        grid=(bsz, t // tm),
        in_specs=[
            pl.BlockSpec((1, tm, B_COLS), lambda b, i: (b, i, 0)),
            pl.BlockSpec((tm, nqk), lambda b, i: (i, 0)),
            pl.BlockSpec((tm, nqk), lambda b, i: (i, 0)),
            _const_spec((1, nqk)), _const_spec((nqk, nqk)),
        ],
        out_specs=[hspec(ATT_Q_HEADS), hspec(ATT_KV_HEADS), hspec(ATT_KV_HEADS)],
        out_shape=[jax.ShapeDtypeStruct((bsz, nh, t, HEAD_DIM), BF16)
                   for nh in (ATT_Q_HEADS, ATT_KV_HEADS, ATT_KV_HEADS)],
        compiler_params=_params(2),
        name="attn_prep",
    )(pb, cos_t, sin_t, gain, bavg)


def _attn_kernel(q_ref, k_ref, v_ref, o_ref, *, tq, tc, q_first):
    qi = pl.program_id(2) + q_first
    q = q_ref[0].reshape(ATT_GROUP * tq, HEAD_DIM)

    def attend(k, v):
        s = _dot_nt(q, k)
        p = jnp.exp(s - jnp.max(s, -1, keepdims=True))
        o = _dot(p.astype(BF16), v) / jnp.sum(p, -1, keepdims=True)
        for g in range(ATT_GROUP):
            o_ref[0, 0, :, g * HEAD_DIM:(g + 1) * HEAD_DIM] = o[g * tq:(g + 1) * tq].astype(BF16)

    @pl.when(qi < tc // tq)
    def _():
        attend(k_ref[0, 0, :tc, :], v_ref[0, 0, :tc, :])

    @pl.when(qi >= tc // tq)
    def _():
        attend(k_ref[0, 0], v_ref[0, 0])


def _attention(q, k, v, tc, need_ctx, tq=128):
    bsz, _, t, _ = q.shape
    q_first = 0 if need_ctx else tc // tq
    nq = t // tq - q_first
    gw = ATT_GROUP * HEAD_DIM
    return pl.pallas_call(
        functools.partial(_attn_kernel, tq=tq, tc=tc, q_first=q_first),
        grid=(bsz, ATT_KV_HEADS, nq),
        in_specs=[
            pl.BlockSpec((1, ATT_GROUP, tq, HEAD_DIM), lambda b, j, i: (b, j, i + q_first, 0)),
            pl.BlockSpec((1, 1, t, HEAD_DIM), lambda b, j, i: (b, j, 0, 0)),
            pl.BlockSpec((1, 1, t, HEAD_DIM), lambda b, j, i: (b, j, 0, 0)),
        ],
        out_specs=pl.BlockSpec((1, 1, tq, gw), lambda b, j, i: (b, j, i + q_first, 0)),
        out_shape=jax.ShapeDtypeStruct((bsz, ATT_KV_HEADS, t, gw), BF16),
        compiler_params=_params(3),
        name="attention",
    )(q, k, v)


HALO = SUBLANES


def _dn_prep_kernel(x_ref, xp_ref, xn_ref, pg_ref, cw_ref, bsum_ref, al_ref, dt_ref,
                    q_out, k_out, v_out, g_out, xe_ref, *, tm, ncb, nblk):
    i = pl.program_id(1)
    seg_start = jnp.logical_or(i == 0, i == ncb)
    seg_end = jnp.logical_or(i == ncb - 1, i == nblk - 1)
    xe_ref[0:HALO, :] = jnp.where(seg_start, 0.0, xp_ref[0])
    xe_ref[HALO:HALO + tm, :] = x_ref[0]
    xe_ref[HALO + tm:2 * HALO + tm, :] = jnp.where(seg_end, 0.0, xn_ref[0])
    first = HALO - DN_CONV // 2
    acc = cw_ref[0:1, :] * xe_ref[first:first + tm, :]
    for j in range(1, DN_CONV):
        acc = acc + cw_ref[j:j + 1, :] * xe_ref[first + j:first + j + tm, :]
    y = _silu(acc)
    qk = y[:, :2 * DN_WIDTH]
    ss = _dot_lhs_split(qk * qk, bsum_ref[...])
    qk = qk * lax.rsqrt(ss + EPS)
    q_out[0] = qk[:, :DN_WIDTH] * (HEAD_DIM ** -0.5)
    k_out[0] = qk[:, DN_WIDTH:]
    v_out[0] = y[:, 2 * DN_WIDTH:]
    pg = pg_ref[0]
    z = pg + dt_ref[...]
    softplus = jnp.maximum(z, 0.0) + jnp.log1p(jnp.exp(-jnp.abs(z)))
    lane = lax.broadcasted_iota(jnp.int32, pg.shape, 1)
    g_out[0] = jnp.where(lane < 2 * DN_HEADS, -jnp.exp(al_ref[...]) * softplus, _sigmoid(pg))


def _dn_prep(pq, pg, conv_w, a_log, dt_bias, tm, ncb):
    bsz, t, c = pq.shape
    nblk = t // tm
    hb = tm // HALO
    pad = lambda v: jnp.zeros((1, LANES), F32).at[0, :2 * DN_HEADS].set(v.reshape(-1))
    bsum = _block_diag_ones(2 * DN_WIDTH, HEAD_DIM)
    rspec = lambda w: pl.BlockSpec((1, tm, w), lambda b, i: (b, i, 0))
    return pl.pallas_call(
        functools.partial(_dn_prep_kernel, tm=tm, ncb=ncb, nblk=nblk),
        grid=(bsz, nblk),
        in_specs=[
            rspec(c),
            pl.BlockSpec((1, HALO, c), lambda b, i: (b, jnp.maximum(i * hb - 1, 0), 0)),
            pl.BlockSpec((1, HALO, c), lambda b, i: (b, jnp.minimum((i + 1) * hb, nblk * hb - 1), 0)),
            rspec(LANES),
            _const_spec(conv_w.shape), _const_spec(bsum.shape),
            _const_spec((1, LANES)), _const_spec((1, LANES)),
        ],
        out_specs=[rspec(DN_WIDTH), rspec(DN_WIDTH), rspec(DN_WIDTH), rspec(LANES)],
        out_shape=[jax.ShapeDtypeStruct((bsz, t, w), F32) for w in (DN_WIDTH, DN_WIDTH, DN_WIDTH, LANES)],
        scratch_shapes=[pltpu.VMEM((tm + 2 * HALO, c), F32)],
        compiler_params=_params(2),
        name="dn_prep",
    )(pq, pq, pq, pg, conv_w, bsum, pad(a_log), pad(dt_bias))


TILE_A = 4 * HEAD_DIM


def _split3(x):
    hi = x.astype(BF16)
    r = x - hi.astype(F32)
    mid = r.astype(BF16)
    lo = (r - mid.astype(F32)).astype(BF16)
    return hi, mid, lo


def _head_id(shape, axis):
    return jnp.right_shift(lax.broadcasted_iota(jnp.int32, shape, axis), 6)


class _Packed:
    def __init__(self):
        c, w = DN_CHUNK, DN_WIDTH
        wb = w - TILE_A
        self.mask_a = _head_id((TILE_A, TILE_A), 0) == _head_id((TILE_A, TILE_A), 1)
        self.mask_b = _head_id((wb, wb), 0) == _head_id((wb, wb), 1)
        self.ri = lax.broadcasted_iota(jnp.int32, (c, w), 0)
        self.cj = lax.broadcasted_iota(jnp.int32, (c, w), 1) & (HEAD_DIM - 1)
        self.eye = (self.ri == self.cj).astype(F32)
        self.incl = (self.cj <= self.ri, self.cj >= self.ri)
        self.strict = (self.cj < self.ri, self.cj > self.ri)
        r64 = lax.broadcasted_iota(jnp.int32, (c, c), 0)
        c64 = lax.broadcasted_iota(jnp.int32, (c, c), 1)
        self.order = ((c64 <= r64).astype(BF16), (c64 >= r64).astype(BF16))
        self.blk = {b: jnp.right_shift(self.ri, int(math.log2(b))) == jnp.right_shift(self.cj, int(math.log2(b)))
                    for b in (8, 16, 32)}

    def weights(self, y):
        y16 = y.astype(BF16)
        ya = jnp.concatenate([y16[:, :TILE_A]] * 4, axis=0)
        yb = jnp.concatenate([y16[:, TILE_A:]] * 2, axis=0)
        zero = jnp.zeros((), BF16)
        return jnp.where(self.mask_a, ya, zero), jnp.where(self.mask_b, yb, zero)

    def dot(self, x, y):
        wa, wb = self.weights(y)
        x16 = x.astype(BF16)
        return jnp.concatenate([_dot(x16[:, :TILE_A], wa), _dot(x16[:, TILE_A:], wb)], axis=-1)

    def dot_nt(self, x, y):
        wa, wb = self.weights(y)
        x16 = x.astype(BF16)
        return jnp.concatenate([_dot_nt(x16[:, :TILE_A], wa), _dot_nt(x16[:, TILE_A:], wb)], axis=-1)

    def dot_tn(self, x, y):
        x16, y16 = x.astype(BF16), y.astype(BF16)
        ta = _dot_tn(x16[:, :TILE_A], y16[:, :TILE_A])
        tb = _dot_tn(x16[:, TILE_A:], y16[:, TILE_A:])
        return jnp.where(self.mask_a, ta, 0.0), jnp.where(self.mask_b, tb, 0.0)

    def tri_inverse(self, lows):
        c = DN_CHUNK
        nb = [-jnp.where(self.blk[8], low, 0.0) for low in lows]
        x = [self.eye + a for a in nb]
        pw = [self.dot(a, a) for a in nb]
        st = [self.dot(jnp.concatenate([xa, p], axis=0), p) for xa, p in zip(x, pw)]
        x = [xa + s[:c] for xa, s in zip(x, st)]
        x = [xa + self.dot(xa, s[c:]) for xa, s in zip(x, st)]
        for b in (8, 16, 32):
            inner = self.blk[b]
            outer = self.blk[2 * b] if 2 * b in self.blk else None
            off = jnp.logical_not(inner) if outer is None else jnp.logical_and(outer, jnp.logical_not(inner))
            xc = [self.dot(xa, jnp.where(off, low, 0.0)) for xa, low in zip(x, lows)]
            x = [xa - self.dot(t, xa) for xa, t in zip(x, xc)]
        return x


def _dn_chunk_terms(pk, dirs, q, k, v, gates, e_g, e_b):
    c = DN_CHUNK
    incl = [pk.incl[d] for d in dirs]
    strict = [pk.strict[d] for d in dirs]
    gc = [sum(_dot(pk.order[d], p) for p in _split3(g)) for d, g in zip(dirs, gates)]
    gcx = [sum(_dot(p, e) for p in _split3(a)) for a, e in zip(gc, e_g)]
    bx = [_dot(g.astype(BF16), e) for g, e in zip(gates, e_b)]
    gtot = [a[c - 1:c, :] if d == 0 else a[0:1, :] for d, a in zip(dirs, gcx)]
    grow = [jnp.sum(pk.eye * a, 0, keepdims=True) for a in gcx]
    decay = [jnp.where(m, jnp.exp(jnp.minimum(a - r, 0.0)), 0.0) for m, a, r in zip(incl, gcx, grow)]
    kbeta = [a * b for a, b in zip(k, bx)]
    kq = [pk.dot_nt(jnp.concatenate([a, b], axis=0), kk) for a, b, kk in zip(kbeta, q, k)]
    low = [jnp.where(m, a[:c] * dc, 0.0) for m, a, dc in zip(strict, kq, decay)]
    intra = [jnp.where(m, a[c:] * dc, 0.0) for m, a, dc in zip(incl, kq, decay)]
    tinv = pk.tri_inverse(low)
    eg = [jnp.exp(a) for a in gcx]
    u = [pk.dot(t, a * b) for t, a, b in zip(tinv, v, bx)]
    wm = [pk.dot(t, a * e) for t, a, e in zip(tinv, kbeta, eg)]
    lhs = [jnp.concatenate([a * e, w_], axis=0).astype(BF16) for a, e, w_ in zip(q, eg, wm)]
    k_dec = [a * jnp.exp(gt - gx) for a, gt, gx in zip(k, gtot, gcx)]
    egl = [jnp.exp(gt) for gt in gtot]
    return [dict(lhs=a, u=b, intra=i, k_dec=kd, egl=e) for a, b, i, kd, e in zip(lhs, u, intra, k_dec, egl)]


def _dn_scan_kernel(qf, kf, vf, gf, qb, kb_, vb, gb, eg_ref, eb_ref, of, ob, sa_ref, sb_ref, *, nsub):
    c = DN_CHUNK

    @pl.when(pl.program_id(1) == 0)
    def _():
        sa_ref[...] = jnp.zeros_like(sa_ref)
        sb_ref[...] = jnp.zeros_like(sb_ref)

    pk = _Packed()
    refs = ((qf, kf, vf, gf, of), (qb, kb_, vb, gb, ob))
    items = [(d, j if d == 0 else nsub - 1 - j) for j in range(nsub) for d in (0, 1)]
    rows = [slice(j * c, (j + 1) * c) for _, j in items]
    dirs = [d for d, _ in items]
    terms = _dn_chunk_terms(
        pk, dirs,
        [refs[d][0][0, r, :] for d, r in zip(dirs, rows)],
        [refs[d][1][0, r, :] for d, r in zip(dirs, rows)],
        [refs[d][2][0, r, :] for d, r in zip(dirs, rows)],
        [refs[d][3][0, r, :] for d, r in zip(dirs, rows)],
        [eg_ref[d] for d in dirs], [eb_ref[d] for d in dirs])
    state = [(sa_ref[d], sb_ref[d]) for d in (0, 1)]
    for d, r, t in zip(dirs, rows, terms):
        sa, sb = state[d]
        lhs = t["lhs"]
        res = jnp.concatenate([_dot(lhs[:, :TILE_A], sa.astype(BF16)), _dot(lhs[:, TILE_A:], sb.astype(BF16))],
                              axis=-1)
        v_new = t["u"] - res[c:]
        refs[d][4][0, r, :] = res[:c] + pk.dot(t["intra"], v_new)
        ta, tb = pk.dot_tn(t["k_dec"], v_new)
        state[d] = (sa * t["egl"][:, :TILE_A] + ta, sb * t["egl"][:, TILE_A:] + tb)
    for d in (0, 1):
        sa_ref[d], sb_ref[d] = state[d]


DN_BLOCK = 2 * DN_CHUNK


def _dn_scan(q, k, v, g, tc):
    bsz, t, w = q.shape
    c = DN_BLOCK
    nct, ncx = t // c, tc // c

    def bwd_blk(s):
        return jnp.where(s < ncx, ncx - 1 - s, nct - 1 - s + ncx)

    lane_head = jnp.arange(w) // HEAD_DIM
    col = jnp.arange(LANES)[:, None]
    expand = lambda first: jnp.stack(
        [(col == first + d * DN_HEADS + lane_head[None, :]).astype(BF16) for d in range(2)])
    fspec = lambda width: pl.BlockSpec((1, c, width), lambda b, s: (b, s, 0))
    bspec = lambda width: pl.BlockSpec((1, c, width), lambda b, s: (b, bwd_blk(s), 0))
    return pl.pallas_call(
        functools.partial(_dn_scan_kernel, nsub=DN_BLOCK // DN_CHUNK),
        grid=(bsz, nct),
        in_specs=[fspec(w), fspec(w), fspec(w), fspec(LANES), bspec(w), bspec(w), bspec(w), bspec(LANES),
                  _const_spec((2, LANES, w)), _const_spec((2, LANES, w))],
        out_specs=[fspec(w), bspec(w)],
        out_shape=[jax.ShapeDtypeStruct((bsz, t, w), F32)] * 2,
        scratch_shapes=[pltpu.VMEM((2, TILE_A, TILE_A), F32), pltpu.VMEM((2, w - TILE_A, w - TILE_A), F32)],
        compiler_params=_params(2),
        name="dn_scan",
    )(q, k, v, g, q, k, v, g, expand(0), expand(2 * DN_HEADS))


def _dn_gate_kernel(of_ref, ob_ref, z_ref, gain_ref, bavg_ref, y_ref):
    o = of_ref[0] + ob_ref[0]
    ms = _dot_lhs_split(o * o, bavg_ref[...])
    y_ref[0] = (o * lax.rsqrt(ms + EPS) * gain_ref[...] * _silu(z_ref[0])).astype(BF16)


def _dn_gate(o_f, o_b, pz, norm_g, tm):
    bsz, t, w = o_f.shape
    gain = jnp.tile(norm_g, DN_HEADS).reshape(1, w)
    bavg = _block_diag_ones(w, HEAD_DIM, 1.0 / HEAD_DIM)
    return pl.pallas_call(
        _dn_gate_kernel,
        grid=(bsz, t // tm),
        in_specs=[
            pl.BlockSpec((1, tm, w), lambda b, i: (b, i, 0)),
            pl.BlockSpec((1, tm, w), lambda b, i: (b, i, 0)),
            pl.BlockSpec((1, tm, w), lambda b, i: (b, i, 0)),
            _const_spec((1, w)), _const_spec((w, w)),
        ],
        out_specs=pl.BlockSpec((1, tm, w), lambda b, i: (b, i, 0)),
        out_shape=jax.ShapeDtypeStruct((bsz, t, w), BF16),
        compiler_params=_params(2),
        name="dn_gate",
    )(o_f, o_b, pz, gain, bavg)


def _post_kernel(x_ref, ya_ref, yb0_ref, yb1_ref, yc_ref, g1_ref, sh2_ref, sc2_ref, g2_ref,
                 wa_ref, wb0_ref, wb1_ref, wc_ref, l1g_ref, l1b_ref, l2g_ref, l2b_ref,
                 wup_ref, wdn_ref, o_ref, *, alpha, f_chunk):
    branch = (_dot(ya_ref[0], wa_ref[...]) + _dot(yb0_ref[0, 0], wb0_ref[...])
              + _dot(yb1_ref[0, 0], wb1_ref[...]) + _dot(yc_ref[0], wc_ref[...]))
    x1 = _norm(alpha * x_ref[0] + g1_ref[...] * branch) * l1g_ref[...] + l1b_ref[...]
    h = (_norm(x1) * (1.0 + sc2_ref[...]) + sh2_ref[...]).astype(BF16)
    d_ff = wup_ref.shape[1]
    m = None
    for f in range(0, d_ff, f_chunk):
        up = jnp.maximum(_dot(h, wup_ref[:, f:f + f_chunk]), 0.0)
        part = _dot((up * up).astype(BF16), wdn_ref[f:f + f_chunk, :])
        m = part if m is None else m + part
    o_ref[0] = _norm(alpha * x1 + g2_ref[...] * m) * l2g_ref[...] + l2b_ref[...]


def _post(xcat, ya, yb, yc, modl, w_out, ln1_g, ln1_b, ln2_g, ln2_b, w_up, w_down, tm, ncb, ctx_row, alpha,
          need_ctx):
    bsz, t, d = xcat.shape
    first = 0 if need_ctx else ncb
    gw = ATT_GROUP * HEAD_DIM
    o1 = A_WIDTH
    wa, wb0, wb1, wc = w_out[:o1], w_out[o1:o1 + gw], w_out[o1 + gw:o1 + 2 * gw], w_out[o1 + 2 * gw:]
    mspec = lambda c: pl.BlockSpec(
        (None, None, 1, d), lambda b, i: (jnp.where(i + first < ncb, ctx_row, b), c, 0, 0))
    rspec = lambda w: pl.BlockSpec((1, tm, w), lambda b, i: (b, i + first, 0))
    vec = lambda a: a.reshape(1, d)
    wspec = lambda a: pl.BlockSpec(a.shape, lambda b, i: (0, 0), pipeline_mode=pl.Buffered(1))
    weights = [w.astype(BF16) for w in (wa, wb0, wb1, wc)]
    w_up, w_down = w_up.astype(BF16), w_down.astype(BF16)
    return pl.pallas_call(
        functools.partial(_post_kernel, alpha=alpha, f_chunk=1024),
        grid=(bsz, t // tm - first),
        in_specs=[
            rspec(d), rspec(A_WIDTH),
            pl.BlockSpec((1, 1, tm, gw), lambda b, i: (b, 0, i + first, 0)),
            pl.BlockSpec((1, 1, tm, gw), lambda b, i: (b, 1, i + first, 0)),
            rspec(DN_WIDTH),
            mspec(2), mspec(3), mspec(4), mspec(5),
            *[wspec(w) for w in weights],
            _const_spec((1, d)), _const_spec((1, d)), _const_spec((1, d)), _const_spec((1, d)),
            wspec(w_up), wspec(w_down),
        ],
        out_specs=pl.BlockSpec((1, tm, d), lambda b, i: (b, i, 0)),
        out_shape=jax.ShapeDtypeStruct((bsz, t - first * tm, d), F32),
        compiler_params=_params(2),
        name="post_mixer",
    )(xcat, ya, yb, yb, yc, modl, modl, modl, modl, *weights,
      vec(ln1_g), vec(ln1_b), vec(ln2_g), vec(ln2_b), w_up, w_down)


def _rope_tables(tc, tl):
    pos = jnp.arange(tl)
    row = (pos // GRID_W).astype(F32)
    col = (pos % GRID_W).astype(F32)
    half = HEAD_DIM // 2
    inv = 1.0 / (ROPE_THETA ** (jnp.arange(0, half, 2, dtype=F32) / half))
    ar, ac = row[:, None] * inv, col[:, None] * inv
    cos = jnp.concatenate([jnp.cos(ar), jnp.cos(ar), jnp.cos(ac), jnp.cos(ac)], -1)
    sin = jnp.concatenate([-jnp.sin(ar), jnp.sin(ar), -jnp.sin(ac), jnp.sin(ac)], -1)
    cos = jnp.concatenate([jnp.ones((tc, HEAD_DIM), F32), cos], 0)
    sin = jnp.concatenate([jnp.zeros((tc, HEAD_DIM), F32), sin], 0)
    scale = HEAD_DIM ** -0.5
    tile = lambda a: jnp.concatenate([jnp.tile(a, (1, ATT_Q_HEADS)) * scale, jnp.tile(a, (1, ATT_KV_HEADS))], -1)
    return tile(cos), tile(sin)


def kernel(x, c, ctx, c_ctx, mod_w, mod_b, w_in, w_out, gmlp_ln_g, gmlp_ln_b, gmlp_w_s, gmlp_b_s,
           attn_q_g, attn_k_g, dn_conv_w, dn_a_log, dn_dt_bias, dn_norm_g,
           ln1_g, ln1_b, ln2_g, ln2_b, w_up, w_down):
    bsz, tl, d = x.shape
    tc = ctx.shape[1]
    depth = mod_w.shape[0]
    assert bsz < MOD_ROWS and tl % GRID_W == 0 and tc % MLP_CHUNK == 0 and tl % MLP_CHUNK == 0
    tm = 256 if tc % 256 == 0 else MLP_CHUNK
    ncb = tc // tm
    ctx_row = bsz
    alpha = (2 * depth) ** 0.25

    cs = jnp.zeros((MOD_ROWS, d), F32).at[:bsz].set(c).at[ctx_row].set(c_ctx)
    mod = _modulation(cs, mod_w, mod_b).reshape(depth, MOD_ROWS, 6, 1, d)
    cos_t, sin_t = _rope_tables(tc, tl)
    xcat = jnp.concatenate([ctx, x], axis=1)

    for l in range(depth):
        need_ctx = l < depth - 1
        w_in_p = jnp.pad(w_in[l], ((0, 0), (0, IN_COLS_PAD - IN_COLS))).astype(BF16)
        pa, pb, pq, pz, pg = _input_projection(xcat, mod[l], w_in_p, tm, ncb, ctx_row)
        ya = _gmlp(pa, gmlp_ln_g[l], gmlp_ln_b[l], gmlp_w_s[l], gmlp_b_s[l], tm)
        qa, ka, va = _attn_prep(pb, cos_t, sin_t, attn_q_g[l], attn_k_g[l], tm)
        yb = _attention(qa, ka, va, tc, need_ctx)
        dq, dk, dv, dg = _dn_prep(pq, pg, dn_conv_w[l], dn_a_log[l], dn_dt_bias[l], tm, ncb)
        o_f, o_b = _dn_scan(dq, dk, dv, dg, tc)
        yc = _dn_gate(o_f, o_b, pz, dn_norm_g[l], tm)
        xcat = _post(xcat, ya, yb, yc, mod[l], w_out[l], ln1_g[l], ln1_b[l], ln2_g[l], ln2_b[l],
                     w_up[l], w_down[l], tm, ncb, ctx_row, alpha, need_ctx)
    return xcat
```

```python
import functools
import math

import jax
import jax.numpy as jnp
from jax import lax
from jax.experimental import pallas as pl
from jax.experimental.pallas import tpu as pltpu

F32 = jnp.float32
BF16 = jnp.bfloat16

GRID_W = 64
HEAD_DIM = 64
A_GROUPS = 4
A_WIDTH = A_GROUPS * HEAD_DIM
MLP_CHUNK = 128
ATT_Q_HEADS = 6
ATT_KV_HEADS = 2
ATT_GROUP = ATT_Q_HEADS // ATT_KV_HEADS
ATT_WIDTH = ATT_Q_HEADS * HEAD_DIM
KV_WIDTH = ATT_KV_HEADS * HEAD_DIM
ROPE_THETA = 10000.0
DN_HEADS = 6
DN_WIDTH = DN_HEADS * HEAD_DIM
DN_CONV = 5
DN_CHUNK = 64
A_COLS = 2 * A_WIDTH
B_COLS = ATT_WIDTH + 2 * KV_WIDTH
QKV_COLS = 3 * DN_WIDTH
GATE_COLS = 4 * DN_HEADS
IN_COLS = A_COLS + B_COLS + QKV_COLS + DN_WIDTH + GATE_COLS
LANES = 128
SUBLANES = 8
IN_COLS_PAD = IN_COLS - GATE_COLS + LANES
EPS = 1e-6
MOD_ROWS = 8
VMEM_LIMIT = 56 * 1024 * 1024


def _dot(a, b):
    return jnp.dot(a, b, preferred_element_type=F32)


def _dot_nt(a, b):
    return lax.dot_general(a, b, (((1,), (1,)), ((), ())), preferred_element_type=F32)


def _dot_tn(a, b):
    return lax.dot_general(a, b, (((0,), (0,)), ((), ())), preferred_element_type=F32)


def _split(x):
    hi = x.astype(BF16)
    lo = (x - hi.astype(F32)).astype(BF16)
    return hi, lo


def _dot_x3(a, b):
    ah, al = _split(a)
    bh, bl = _split(b)
    return _dot(ah, bh) + (_dot(ah, bl) + _dot(al, bh))


def _dot_lhs_split(x, m):
    hi, lo = _split(x)
    return _dot(hi, m) + _dot(lo, m)


def _norm(x):
    mu = jnp.mean(x, -1, keepdims=True)
    xc = x - mu
    var = jnp.mean(xc * xc, -1, keepdims=True)
    return xc * lax.rsqrt(var + EPS)


def _sigmoid(x):
    return 1.0 / (1.0 + jnp.exp(-x))


def _silu(x):
    return x * _sigmoid(x)


def _params(n_grid, vmem=VMEM_LIMIT):
    return pltpu.CompilerParams(dimension_semantics=("arbitrary",) * n_grid, vmem_limit_bytes=vmem)


def _const_spec(shape):
    nd = len(shape)
    return pl.BlockSpec(shape, lambda *_: (0,) * nd)


def _mod_kernel(cs_ref, w_ref, b_ref, o_ref):
    cs = cs_ref[...]
    o_ref[0] = _dot_x3(_silu(cs), w_ref[0]) + b_ref[0]


def _modulation(cs, mod_w, mod_b):
    depth, d, n = mod_w.shape
    tn = 1536
    return pl.pallas_call(
        _mod_kernel,
        grid=(depth, n // tn),
        in_specs=[
            pl.BlockSpec((MOD_ROWS, d), lambda l, j: (0, 0)),
            pl.BlockSpec((1, d, tn), lambda l, j: (l, 0, j)),
            pl.BlockSpec((1, 1, tn), lambda l, j: (l, 0, j)),
        ],
        out_specs=pl.BlockSpec((1, MOD_ROWS, tn), lambda l, j: (l, 0, j)),
        out_shape=jax.ShapeDtypeStruct((depth, MOD_ROWS, n), F32),
        compiler_params=_params(2),
        name="modulation",
    )(cs, mod_w, mod_b.reshape(depth, 1, n))


def _pre_kernel(x_ref, sh_ref, sc_ref, w_ref, pa_ref, pb_ref, pq_ref, pz_ref, pg_ref):
    h = _norm(x_ref[0]) * (1.0 + sc_ref[...]) + sh_ref[...]
    p = _dot(h.astype(BF16), w_ref[...])
    o = 0
    for ref in (pa_ref, pb_ref, pq_ref, pz_ref, pg_ref):
        w = ref.shape[-1]
        ref[0] = p[:, o:o + w]
        o += w


def _input_projection(xcat, modl, w_in_p, tm, ncb, ctx_row):
    bsz, t, d = xcat.shape
    widths = (A_COLS, B_COLS, QKV_COLS, DN_WIDTH, LANES)
    mspec = lambda c: pl.BlockSpec((None, None, 1, d), lambda b, i: (jnp.where(i < ncb, ctx_row, b), c, 0, 0))
    return pl.pallas_call(
        _pre_kernel,
        grid=(bsz, t // tm),
        in_specs=[
            pl.BlockSpec((1, tm, d), lambda b, i: (b, i, 0)),
            mspec(0), mspec(1),
            _const_spec(w_in_p.shape),
        ],
        out_specs=[pl.BlockSpec((1, tm, w), lambda b, i: (b, i, 0)) for w in widths],
        out_shape=[jax.ShapeDtypeStruct((bsz, t, w), F32) for w in widths],
        compiler_params=_params(2),
        name="input_projection",
    )(xcat, modl, modl, w_in_p)


def _gmlp_kernel(pa_ref, lng_ref, lnb_ref, ws_ref, bias_ref, o_ref, *, nchunk):
    for ci in range(nchunk):
        rows = slice(ci * MLP_CHUNK, (ci + 1) * MLP_CHUNK)
        p = pa_ref[0, rows, :]
        a = 0.5 * p * (1.0 + lax.erf(p * (2.0 ** -0.5)))
        u = a[:, :A_WIDTH]
        v = _norm(a[:, A_WIDTH:]) * lng_ref[...] + lnb_ref[...]
        vb = v.astype(BF16)
        mixed = jnp.concatenate(
            [_dot(ws_ref[g], vb[:, g * HEAD_DIM:(g + 1) * HEAD_DIM]) for g in range(A_GROUPS)], axis=-1)
        o_ref[0, rows, :] = (u * (mixed + bias_ref[...])).astype(BF16)


def _gmlp(pa, ln_g, ln_b, w_s, b_s, tm):
    bsz, t, _ = pa.shape
    bias = jnp.repeat(b_s.T, HEAD_DIM, axis=1)
    return pl.pallas_call(
        functools.partial(_gmlp_kernel, nchunk=tm // MLP_CHUNK),
        grid=(bsz, t // tm),
        in_specs=[
            pl.BlockSpec((1, tm, A_COLS), lambda b, i: (b, i, 0)),
            _const_spec((1, A_WIDTH)), _const_spec((1, A_WIDTH)),
            _const_spec(w_s.shape), _const_spec(bias.shape),
        ],
        out_specs=pl.BlockSpec((1, tm, A_WIDTH), lambda b, i: (b, i, 0)),
        out_shape=jax.ShapeDtypeStruct((bsz, t, A_WIDTH), BF16),
        compiler_params=_params(2),
        name="gmlp",
    )(pa, ln_g.reshape(1, -1), ln_b.reshape(1, -1), w_s.astype(BF16), bias)


def _swap_rope_pairs(x):
    n = x.shape[-1]
    lane = lax.broadcasted_iota(jnp.int32, x.shape, x.ndim - 1)
    first = (lane & 31) < 16
    return jnp.where(first, pltpu.roll(x, n - 16, x.ndim - 1), pltpu.roll(x, 16, x.ndim - 1))


def _attn_prep_kernel(pb_ref, cos_ref, sin_ref, gain_ref, bavg_ref, q_out, k_out, v_out):
    pb = pb_ref[0]
    nqk = ATT_WIDTH + KV_WIDTH
    qk = pb[:, :nqk]
    ms = _dot_lhs_split(qk * qk, bavg_ref[...])
    qk = qk * lax.rsqrt(ms + EPS) * gain_ref[...]
    r = qk * cos_ref[...] + _swap_rope_pairs(qk) * sin_ref[...]
    for h in range(ATT_Q_HEADS):
        q_out[0, h] = r[:, h * HEAD_DIM:(h + 1) * HEAD_DIM].astype(BF16)
    vv = pb[:, nqk:nqk + KV_WIDTH]
    lane = lax.broadcasted_iota(jnp.int32, vv.shape, 1)
    for h in range(ATT_KV_HEADS):
        k_out[0, h] = r[:, ATT_WIDTH + h * HEAD_DIM:ATT_WIDTH + (h + 1) * HEAD_DIM].astype(BF16)
        vh = vv if h == 0 else pltpu.roll(vv, KV_WIDTH - h * HEAD_DIM, 1)
        v_out[0, h] = jnp.where(lane < HEAD_DIM, vh, jnp.where(lane == HEAD_DIM, 1.0, 0.0)).astype(BF16)


def _block_diag_ones(n, blk, value=1.0):
    idx = jnp.arange(n) // blk
    return jnp.where(idx[:, None] == idx[None, :], value, 0.0).astype(BF16)


def _attn_prep(pb, cos_t, sin_t, q_g, k_g, tm):
    bsz, t, _ = pb.shape
    nqk = ATT_WIDTH + KV_WIDTH
    gain = jnp.concatenate([jnp.tile(q_g, ATT_Q_HEADS), jnp.tile(k_g, ATT_KV_HEADS)]).reshape(1, nqk)
    bavg = _block_diag_ones(nqk, HEAD_DIM, 1.0 / HEAD_DIM)
    hspec = lambda nh, w: pl.BlockSpec((1, nh, tm, w), lambda b, i: (b, 0, i, 0))
    outs = ((ATT_Q_HEADS, HEAD_DIM), (ATT_KV_HEADS, HEAD_DIM), (ATT_KV_HEADS, LANES))
    return pl.pallas_call(
        _attn_prep_kernel,
        grid=(bsz, t // tm),
        in_specs=[
            pl.BlockSpec((1, tm, B_COLS), lambda b, i: (b, i, 0)),
            pl.BlockSpec((tm, nqk), lambda b, i: (i, 0)),
            pl.BlockSpec((tm, nqk), lambda b, i: (i, 0)),
            _const_spec((1, nqk)), _const_spec((nqk, nqk)),
        ],
        out_specs=[hspec(nh, w) for nh, w in outs],
        out_shape=[jax.ShapeDtypeStruct((bsz, nh, t, w), BF16) for nh, w in outs],
        compiler_params=_params(2),
        name="attn_prep",
    )(pb, cos_t, sin_t, gain, bavg)


KV_BLOCK = 256


def _key_blocks(nkeys):
    return [(s0, min(KV_BLOCK, nkeys - s0)) for s0 in range(0, nkeys, KV_BLOCK)]


def _store_heads(o_ref, acc, tq):
    o = acc[:, :HEAD_DIM] / acc[:, HEAD_DIM:HEAD_DIM + 1]
    for g in range(ATT_GROUP):
        o_ref[0, 0, :, g * HEAD_DIM:(g + 1) * HEAD_DIM] = o[g * tq:(g + 1) * tq].astype(BF16)


def _attn_lat_kernel(q_ref, k_ref, v_ref, o_ref, s_ref, mx_ref, *, tq):
    i = pl.program_id(2)
    cur = i % 2
    prev = 1 - cur
    t = k_ref.shape[2]

    @pl.when(i == 0)
    def _():
        s_ref[1] = jnp.zeros(s_ref.shape[1:], F32)
        mx_ref[1] = jnp.zeros(mx_ref.shape[1:], F32)

    q = q_ref[0].reshape(ATT_GROUP * tq, HEAD_DIM)
    m = jnp.max(mx_ref[prev], -1, keepdims=True)
    mx = None
    acc = None
    for s0, n in _key_blocks(t):
        s = _dot_nt(q, k_ref[0, 0, s0:s0 + n, :])
        s_ref[cur, :, s0:s0 + n] = s
        for l0 in range(0, n, LANES):
            part = s[:, l0:l0 + LANES]
            mx = part if mx is None else jnp.maximum(mx, part)
        p = jnp.exp2(s_ref[prev, :, s0:s0 + n] - m).astype(BF16)
        part = _dot(p, v_ref[0, 0, s0:s0 + n, :])
        acc = part if acc is None else acc + part
    mx_ref[cur] = mx
    _store_heads(o_ref, acc, tq)


def _attn_ctx_kernel(q_ref, k_ref, v_ref, o_ref, *, tc):
    q = q_ref[0].reshape(ATT_GROUP * tc, HEAD_DIM)
    s = _dot_nt(q, k_ref[0, 0])
    p = jnp.exp2(s - jnp.max(s, -1, keepdims=True)).astype(BF16)
    _store_heads(o_ref, _dot(p, v_ref[0, 0]), tc)


def _attention(q, k, v, tc, need_ctx, tq=128):
    bsz, _, t, _ = q.shape
    gw = ATT_GROUP * HEAD_DIM
    first = tc // tq
    nq = (t - tc) // tq
    y_lat = pl.pallas_call(
        functools.partial(_attn_lat_kernel, tq=tq),
        grid=(bsz, ATT_KV_HEADS, nq + 1),
        in_specs=[
            pl.BlockSpec((1, ATT_GROUP, tq, HEAD_DIM), lambda b, j, i: (b, j, first + jnp.minimum(i, nq - 1), 0)),
            pl.BlockSpec((1, 1, t, HEAD_DIM), lambda b, j, i: (b, j, 0, 0)),
            pl.BlockSpec((1, 1, t, LANES), lambda b, j, i: (b, j, 0, 0)),
        ],
        out_specs=pl.BlockSpec((1, 1, tq, gw), lambda b, j, i: (b, j, jnp.maximum(i - 1, 0), 0)),
        out_shape=jax.ShapeDtypeStruct((bsz, ATT_KV_HEADS, t - tc, gw), BF16),
        scratch_shapes=[pltpu.VMEM((2, ATT_GROUP * tq, t), F32), pltpu.VMEM((2, ATT_GROUP * tq, LANES), F32)],
        compiler_params=_params(3),
        name="attention",
    )(q, k, v)
    if not need_ctx:
        return None, y_lat
    y_ctx = pl.pallas_call(
        functools.partial(_attn_ctx_kernel, tc=tc),
        grid=(bsz, ATT_KV_HEADS),
        in_specs=[
            pl.BlockSpec((1, ATT_GROUP, tc, HEAD_DIM), lambda b, j: (b, j, 0, 0)),
            pl.BlockSpec((1, 1, tc, HEAD_DIM), lambda b, j: (b, j, 0, 0)),
            pl.BlockSpec((1, 1, tc, LANES), lambda b, j: (b, j, 0, 0)),
        ],
        out_specs=pl.BlockSpec((1, 1, tc, gw), lambda b, j: (b, j, 0, 0)),
        out_shape=jax.ShapeDtypeStruct((bsz, ATT_KV_HEADS, tc, gw), BF16),
        compiler_params=_params(2),
        name="attention_ctx",
    )(q, k, v)
    return y_ctx, y_lat


HALO = SUBLANES


def _dn_prep_kernel(x_ref, xp_ref, xn_ref, pg_ref, cw_ref, bsum_ref, al_ref, dt_ref,
                    q_out, k_out, v_out, g_out, xe_ref, *, tm, ncb, nblk):
    i = pl.program_id(1)
    seg_start = jnp.logical_or(i == 0, i == ncb)
    seg_end = jnp.logical_or(i == ncb - 1, i == nblk - 1)
    xe_ref[0:HALO, :] = jnp.where(seg_start, 0.0, xp_ref[0])
    xe_ref[HALO:HALO + tm, :] = x_ref[0]
    xe_ref[HALO + tm:2 * HALO + tm, :] = jnp.where(seg_end, 0.0, xn_ref[0])
    first = HALO - DN_CONV // 2
    acc = cw_ref[0:1, :] * xe_ref[first:first + tm, :]
    for j in range(1, DN_CONV):
        acc = acc + cw_ref[j:j + 1, :] * xe_ref[first + j:first + j + tm, :]
    y = _silu(acc)
    qk = y[:, :2 * DN_WIDTH]
    ss = _dot_lhs_split(qk * qk, bsum_ref[...])
    qk = qk * lax.rsqrt(ss + EPS)
    q_out[0] = qk[:, :DN_WIDTH] * (HEAD_DIM ** -0.5)
    k_out[0] = qk[:, DN_WIDTH:]
    v_out[0] = y[:, 2 * DN_WIDTH:]
    pg = pg_ref[0]
    z = pg + dt_ref[...]
    softplus = jnp.maximum(z, 0.0) + jnp.log1p(jnp.exp(-jnp.abs(z)))
    lane = lax.broadcasted_iota(jnp.int32, pg.shape, 1)
    g_out[0] = jnp.where(lane < 2 * DN_HEADS, -jnp.exp(al_ref[...]) * softplus, _sigmoid(pg))


def _dn_prep(pq, pg, conv_w, a_log, dt_bias, tm, ncb):
    bsz, t, c = pq.shape
    nblk = t // tm
    hb = tm // HALO
    pad = lambda v: jnp.zeros((1, LANES), F32).at[0, :2 * DN_HEADS].set(v.reshape(-1))
    bsum = _block_diag_ones(2 * DN_WIDTH, HEAD_DIM)
    rspec = lambda w: pl.BlockSpec((1, tm, w), lambda b, i: (b, i, 0))
    return pl.pallas_call(
        functools.partial(_dn_prep_kernel, tm=tm, ncb=ncb, nblk=nblk),
        grid=(bsz, nblk),
        in_specs=[
            rspec(c),
            pl.BlockSpec((1, HALO, c), lambda b, i: (b, jnp.maximum(i * hb - 1, 0), 0)),
            pl.BlockSpec((1, HALO, c), lambda b, i: (b, jnp.minimum((i + 1) * hb, nblk * hb - 1), 0)),
            rspec(LANES),
            _const_spec(conv_w.shape), _const_spec(bsum.shape),
            _const_spec((1, LANES)), _const_spec((1, LANES)),
        ],
        out_specs=[rspec(DN_WIDTH), rspec(DN_WIDTH), rspec(DN_WIDTH), rspec(LANES)],
        out_shape=[jax.ShapeDtypeStruct((bsz, t, w), F32) for w in (DN_WIDTH, DN_WIDTH, DN_WIDTH, LANES)],
        scratch_shapes=[pltpu.VMEM((tm + 2 * HALO, c), F32)],
        compiler_params=_params(2),
        name="dn_prep",
    )(pq, pq, pq, pg, conv_w, bsum, pad(a_log), pad(dt_bias))


TILE_A = 4 * HEAD_DIM


def _split3(x):
    hi = x.astype(BF16)
    r = x - hi.astype(F32)
    mid = r.astype(BF16)
    lo = (r - mid.astype(F32)).astype(BF16)
    return hi, mid, lo


def _head_id(shape, axis):
    return jnp.right_shift(lax.broadcasted_iota(jnp.int32, shape, axis), 6)


class _Packed:
    def __init__(self):
        c, w = DN_CHUNK, DN_WIDTH
        wb = w - TILE_A
        self.mask_a = _head_id((TILE_A, TILE_A), 0) == _head_id((TILE_A, TILE_A), 1)
        self.mask_b = _head_id((wb, wb), 0) == _head_id((wb, wb), 1)
        self.ri = lax.broadcasted_iota(jnp.int32, (c, w), 0)
        self.cj = lax.broadcasted_iota(jnp.int32, (c, w), 1) & (HEAD_DIM - 1)
        self.eye = (self.ri == self.cj).astype(F32)
        self.incl = (self.cj <= self.ri, self.cj >= self.ri)
        self.strict = (self.cj < self.ri, self.cj > self.ri)
        r64 = lax.broadcasted_iota(jnp.int32, (c, c), 0)
        c64 = lax.broadcasted_iota(jnp.int32, (c, c), 1)
        self.order = ((c64 <= r64).astype(BF16), (c64 >= r64).astype(BF16))
        self.blk = {b: jnp.right_shift(self.ri, int(math.log2(b))) == jnp.right_shift(self.cj, int(math.log2(b)))
                    for b in (8, 16, 32)}

    def weights(self, y):
        y16 = y.astype(BF16)
        ya = jnp.concatenate([y16[:, :TILE_A]] * 4, axis=0)
        yb = jnp.concatenate([y16[:, TILE_A:]] * 2, axis=0)
        zero = jnp.zeros((), BF16)
        return jnp.where(self.mask_a, ya, zero), jnp.where(self.mask_b, yb, zero)

    def dot(self, x, y):
        wa, wb = self.weights(y)
        x16 = x.astype(BF16)
        return jnp.concatenate([_dot(x16[:, :TILE_A], wa), _dot(x16[:, TILE_A:], wb)], axis=-1)

    def dot_nt(self, x, y):
        wa, wb = self.weights(y)
        x16 = x.astype(BF16)
        return jnp.concatenate([_dot_nt(x16[:, :TILE_A], wa), _dot_nt(x16[:, TILE_A:], wb)], axis=-1)

    def dot_tn(self, x, y):
        x16, y16 = x.astype(BF16), y.astype(BF16)
        ta = _dot_tn(x16[:, :TILE_A], y16[:, :TILE_A])
        tb = _dot_tn(x16[:, TILE_A:], y16[:, TILE_A:])
        return jnp.where(self.mask_a, ta, 0.0), jnp.where(self.mask_b, tb, 0.0)

    def tri_inverse(self, lows):
        c = DN_CHUNK
        nb = [-jnp.where(self.blk[8], low, 0.0) for low in lows]
        x = [self.eye + a for a in nb]
        pw = [self.dot(a, a) for a in nb]
        st = [self.dot(jnp.concatenate([xa, p], axis=0), p) for xa, p in zip(x, pw)]
        x = [xa + s[:c] for xa, s in zip(x, st)]
        x = [xa + self.dot(xa, s[c:]) for xa, s in zip(x, st)]
        for b in (8, 16, 32):
            inner = self.blk[b]
            outer = self.blk[2 * b] if 2 * b in self.blk else None
            off = jnp.logical_not(inner) if outer is None else jnp.logical_and(outer, jnp.logical_not(inner))
            xc = [self.dot(xa, jnp.where(off, low, 0.0)) for xa, low in zip(x, lows)]
            x = [xa - self.dot(t, xa) for xa, t in zip(x, xc)]
        return x


def _dn_chunk_terms(pk, dirs, q, k, v, gates, e_g, e_b):
    c = DN_CHUNK
    incl = [pk.incl[d] for d in dirs]
    strict = [pk.strict[d] for d in dirs]
    gc = [sum(_dot(pk.order[d], p) for p in _split3(g)) for d, g in zip(dirs, gates)]
    gcx = [sum(_dot(p, e) for p in _split3(a)) for a, e in zip(gc, e_g)]
    bx = [_dot(g.astype(BF16), e) for g, e in zip(gates, e_b)]
    gtot = [a[c - 1:c, :] if d == 0 else a[0:1, :] for d, a in zip(dirs, gcx)]
    grow = [jnp.sum(pk.eye * a, 0, keepdims=True) for a in gcx]
    decay = [jnp.where(m, jnp.exp(jnp.minimum(a - r, 0.0)), 0.0) for m, a, r in zip(incl, gcx, grow)]
    kbeta = [a * b for a, b in zip(k, bx)]
    kq = [pk.dot_nt(jnp.concatenate([a, b], axis=0), kk) for a, b, kk in zip(kbeta, q, k)]
    low = [jnp.where(m, a[:c] * dc, 0.0) for m, a, dc in zip(strict, kq, decay)]
    intra = [jnp.where(m, a[c:] * dc, 0.0) for m, a, dc in zip(incl, kq, decay)]
    tinv = pk.tri_inverse(low)
    eg = [jnp.exp(a) for a in gcx]
    u = [pk.dot(t, a * b) for t, a, b in zip(tinv, v, bx)]
    wm = [pk.dot(t, a * e) for t, a, e in zip(tinv, kbeta, eg)]
    lhs = [jnp.concatenate([a * e, w_], axis=0).astype(BF16) for a, e, w_ in zip(q, eg, wm)]
    k_dec = [a * jnp.exp(gt - gx) for a, gt, gx in zip(k, gtot, gcx)]
    egl = [jnp.exp(gt) for gt in gtot]
    return [dict(lhs=a, u=b, intra=i, k_dec=kd, egl=e) for a, b, i, kd, e in zip(lhs, u, intra, k_dec, egl)]


def _dn_scan_kernel(qf, kf, vf, gf, qb, kb_, vb, gb, eg_ref, eb_ref, of, ob, sa_ref, sb_ref, *, nsub):
    c = DN_CHUNK

    @pl.when(pl.program_id(1) == 0)
    def _():
        sa_ref[...] = jnp.zeros_like(sa_ref)
        sb_ref[...] = jnp.zeros_like(sb_ref)

    pk = _Packed()
    refs = ((qf, kf, vf, gf, of), (qb, kb_, vb, gb, ob))
    items = [(d, j if d == 0 else nsub - 1 - j) for j in range(nsub) for d in (0, 1)]
    rows = [slice(j * c, (j + 1) * c) for _, j in items]
    dirs = [d for d, _ in items]
    terms = _dn_chunk_terms(
        pk, dirs,
        [refs[d][0][0, r, :] for d, r in zip(dirs, rows)],
        [refs[d][1][0, r, :] for d, r in zip(dirs, rows)],
        [refs[d][2][0, r, :] for d, r in zip(dirs, rows)],
        [refs[d][3][0, r, :] for d, r in zip(dirs, rows)],
        [eg_ref[d] for d in dirs], [eb_ref[d] for d in dirs])
    state = [(sa_ref[d], sb_ref[d]) for d in (0, 1)]
    for d, r, t in zip(dirs, rows, terms):
        sa, sb = state[d]
        lhs = t["lhs"]
        res = jnp.concatenate([_dot(lhs[:, :TILE_A], sa.astype(BF16)), _dot(lhs[:, TILE_A:], sb.astype(BF16))],
                              axis=-1)
        v_new = t["u"] - res[c:]
        refs[d][4][0, r, :] = res[:c] + pk.dot(t["intra"], v_new)
        ta, tb = pk.dot_tn(t["k_dec"], v_new)
        state[d] = (sa * t["egl"][:, :TILE_A] + ta, sb * t["egl"][:, TILE_A:] + tb)
    for d in (0, 1):
        sa_ref[d], sb_ref[d] = state[d]


DN_BLOCK = 2 * DN_CHUNK


def _dn_scan(q, k, v, g, tc):
    bsz, t, w = q.shape
    c = DN_BLOCK
    nct, ncx = t // c, tc // c

    def bwd_blk(s):
        return jnp.where(s < ncx, ncx - 1 - s, nct - 1 - s + ncx)

    lane_head = jnp.arange(w) // HEAD_DIM
    col = jnp.arange(LANES)[:, None]
    expand = lambda first: jnp.stack(
        [(col == first + d * DN_HEADS + lane_head[None, :]).astype(BF16) for d in range(2)])
    fspec = lambda width: pl.BlockSpec((1, c, width), lambda b, s: (b, s, 0))
    bspec = lambda width: pl.BlockSpec((1, c, width), lambda b, s: (b, bwd_blk(s), 0))
    return pl.pallas_call(
        functools.partial(_dn_scan_kernel, nsub=DN_BLOCK // DN_CHUNK),
        grid=(bsz, nct),
        in_specs=[fspec(w), fspec(w), fspec(w), fspec(LANES), bspec(w), bspec(w), bspec(w), bspec(LANES),
                  _const_spec((2, LANES, w)), _const_spec((2, LANES, w))],
        out_specs=[fspec(w), bspec(w)],
        out_shape=[jax.ShapeDtypeStruct((bsz, t, w), F32)] * 2,
        scratch_shapes=[pltpu.VMEM((2, TILE_A, TILE_A), F32), pltpu.VMEM((2, w - TILE_A, w - TILE_A), F32)],
        compiler_params=_params(2),
        name="dn_scan",
    )(q, k, v, g, q, k, v, g, expand(0), expand(2 * DN_HEADS))


def _dn_gate_kernel(of_ref, ob_ref, z_ref, gain_ref, bavg_ref, y_ref):
    o = of_ref[0] + ob_ref[0]
    ms = _dot_lhs_split(o * o, bavg_ref[...])
    y_ref[0] = (o * lax.rsqrt(ms + EPS) * gain_ref[...] * _silu(z_ref[0])).astype(BF16)


def _dn_gate(o_f, o_b, pz, norm_g, tm):
    bsz, t, w = o_f.shape
    gain = jnp.tile(norm_g, DN_HEADS).reshape(1, w)
    bavg = _block_diag_ones(w, HEAD_DIM, 1.0 / HEAD_DIM)
    return pl.pallas_call(
        _dn_gate_kernel,
        grid=(bsz, t // tm),
        in_specs=[
            pl.BlockSpec((1, tm, w), lambda b, i: (b, i, 0)),
            pl.BlockSpec((1, tm, w), lambda b, i: (b, i, 0)),
            pl.BlockSpec((1, tm, w), lambda b, i: (b, i, 0)),
            _const_spec((1, w)), _const_spec((w, w)),
        ],
        out_specs=pl.BlockSpec((1, tm, w), lambda b, i: (b, i, 0)),
        out_shape=jax.ShapeDtypeStruct((bsz, t, w), BF16),
        compiler_params=_params(2),
        name="dn_gate",
    )(o_f, o_b, pz, gain, bavg)


def _post_kernel(x_ref, ya_ref, yb0_ref, yb1_ref, yc_ref, g1_ref, sh2_ref, sc2_ref, g2_ref,
                 wa_ref, wb0_ref, wb1_ref, wc_ref, l1g_ref, l1b_ref, l2g_ref, l2b_ref,
                 wup_ref, wdn_ref, o_ref, *, alpha, f_chunk):
    branch = (_dot(ya_ref[0], wa_ref[...]) + _dot(yb0_ref[0, 0], wb0_ref[...])
              + _dot(yb1_ref[0, 0], wb1_ref[...]) + _dot(yc_ref[0], wc_ref[...]))
    x1 = _norm(alpha * x_ref[0] + g1_ref[...] * branch) * l1g_ref[...] + l1b_ref[...]
    h = (_norm(x1) * (1.0 + sc2_ref[...]) + sh2_ref[...]).astype(BF16)
    d_ff = wup_ref.shape[1]
    m = None
    for f in range(0, d_ff, f_chunk):
        up = jnp.maximum(_dot(h, wup_ref[:, f:f + f_chunk]), 0.0)
        part = _dot((up * up).astype(BF16), wdn_ref[f:f + f_chunk, :])
        m = part if m is None else m + part
    o_ref[0] = _norm(alpha * x1 + g2_ref[...] * m) * l2g_ref[...] + l2b_ref[...]


def _post(xcat, ya, yb, yc, modl, w_out, ln1_g, ln1_b, ln2_g, ln2_b, w_up, w_down, tm, ncb, ctx_row, alpha,
          need_ctx):
    bsz, t, d = xcat.shape
    first = 0 if need_ctx else ncb
    yb_first = first - (t - yb.shape[2]) // tm
    gw = ATT_GROUP * HEAD_DIM
    o1 = A_WIDTH
    wa, wb0, wb1, wc = w_out[:o1], w_out[o1:o1 + gw], w_out[o1 + gw:o1 + 2 * gw], w_out[o1 + 2 * gw:]
    mspec = lambda c: pl.BlockSpec(
        (None, None, 1, d), lambda b, i: (jnp.where(i + first < ncb, ctx_row, b), c, 0, 0))
    rspec = lambda w: pl.BlockSpec((1, tm, w), lambda b, i: (b, i + first, 0))
    vec = lambda a: a.reshape(1, d)
    wspec = lambda a: pl.BlockSpec(a.shape, lambda b, i: (0, 0), pipeline_mode=pl.Buffered(1))
    weights = [w.astype(BF16) for w in (wa, wb0, wb1, wc)]
    w_up, w_down = w_up.astype(BF16), w_down.astype(BF16)
    return pl.pallas_call(
        functools.partial(_post_kernel, alpha=alpha, f_chunk=1024),
        grid=(bsz, t // tm - first),
        in_specs=[
            rspec(d), rspec(A_WIDTH),
            pl.BlockSpec((1, 1, tm, gw), lambda b, i: (b, 0, i + yb_first, 0)),
            pl.BlockSpec((1, 1, tm, gw), lambda b, i: (b, 1, i + yb_first, 0)),
            rspec(DN_WIDTH),
            mspec(2), mspec(3), mspec(4), mspec(5),
            *[wspec(w) for w in weights],
            _const_spec((1, d)), _const_spec((1, d)), _const_spec((1, d)), _const_spec((1, d)),
            wspec(w_up), wspec(w_down),
        ],
        out_specs=pl.BlockSpec((1, tm, d), lambda b, i: (b, i, 0)),
        out_shape=jax.ShapeDtypeStruct((bsz, t - first * tm, d), F32),
        compiler_params=_params(2),
        name="post_mixer",
    )(xcat, ya, yb, yb, yc, modl, modl, modl, modl, *weights,
      vec(ln1_g), vec(ln1_b), vec(ln2_g), vec(ln2_b), w_up, w_down)


def _rope_tables(tc, tl):
    pos = jnp.arange(tl)
    row = (pos // GRID_W).astype(F32)
    col = (pos % GRID_W).astype(F32)
    half = HEAD_DIM // 2
    inv = 1.0 / (ROPE_THETA ** (jnp.arange(0, half, 2, dtype=F32) / half))
    ar, ac = row[:, None] * inv, col[:, None] * inv
    cos = jnp.concatenate([jnp.cos(ar), jnp.cos(ar), jnp.cos(ac), jnp.cos(ac)], -1)
    sin = jnp.concatenate([-jnp.sin(ar), jnp.sin(ar), -jnp.sin(ac), jnp.sin(ac)], -1)
    cos = jnp.concatenate([jnp.ones((tc, HEAD_DIM), F32), cos], 0)
    sin = jnp.concatenate([jnp.zeros((tc, HEAD_DIM), F32), sin], 0)
    scale = HEAD_DIM ** -0.5 * math.log2(math.e)
    tile =lambda a: jnp.concatenate([jnp.tile(a, (1, ATT_Q_HEADS)) * scale, jnp.tile(a, (1, ATT_KV_HEADS))], -1)
    return tile(cos), tile(sin)


def kernel(x, c, ctx, c_ctx, mod_w, mod_b, w_in, w_out, gmlp_ln_g, gmlp_ln_b, gmlp_w_s, gmlp_b_s,
           attn_q_g, attn_k_g, dn_conv_w, dn_a_log, dn_dt_bias, dn_norm_g,
           ln1_g, ln1_b, ln2_g, ln2_b, w_up, w_down):
    bsz, tl, d = x.shape
    tc = ctx.shape[1]
    depth = mod_w.shape[0]
    assert bsz < MOD_ROWS and tl % GRID_W == 0 and tc % MLP_CHUNK == 0 and tl % MLP_CHUNK == 0
    tm = 256 if tc % 256 == 0 else MLP_CHUNK
    ncb = tc // tm
    ctx_row = bsz
    alpha = (2 * depth) ** 0.25

    cs = jnp.zeros((MOD_ROWS, d), F32).at[:bsz].set(c).at[ctx_row].set(c_ctx)
    mod = _modulation(cs, mod_w, mod_b).reshape(depth, MOD_ROWS, 6, 1, d)
    cos_t, sin_t = _rope_tables(tc, tl)
    xcat = jnp.concatenate([ctx, x], axis=1)

    for l in range(depth):
        need_ctx = l < depth - 1
        w_in_p = jnp.pad(w_in[l], ((0, 0), (0, IN_COLS_PAD - IN_COLS))).astype(BF16)
        pa, pb, pq, pz, pg = _input_projection(xcat, mod[l], w_in_p, tm, ncb, ctx_row)
        ya = _gmlp(pa, gmlp_ln_g[l], gmlp_ln_b[l], gmlp_w_s[l], gmlp_b_s[l], tm)
        qa, ka, va = _attn_prep(pb, cos_t, sin_t, attn_q_g[l], attn_k_g[l], tm)
        yb_ctx, yb = _attention(qa, ka, va, tc, need_ctx)
        if need_ctx:
            yb = jnp.concatenate([yb_ctx, yb], axis=2)
        dq, dk, dv, dg = _dn_prep(pq, pg, dn_conv_w[l], dn_a_log[l], dn_dt_bias[l], tm, ncb)
        o_f, o_b = _dn_scan(dq, dk, dv, dg, tc)
        yc = _dn_gate(o_f, o_b, pz, dn_norm_g[l], tm)
        xcat = _post(xcat, ya, yb, yc, mod[l], w_out[l], ln1_g[l], ln1_b[l], ln2_g[l], ln2_b[l],
                     w_up[l], w_down[l], tm, ncb, ctx_row, alpha, need_ctx)
    return xcat
```

```python
import functools
import math

import jax
import jax.numpy as jnp
from jax import lax
from jax.experimental import pallas as pl
from jax.experimental.pallas import tpu as pltpu

F32 = jnp.float32
BF16 = jnp.bfloat16

GRID_W = 64
HEAD_DIM = 64
A_GROUPS = 4
A_WIDTH = A_GROUPS * HEAD_DIM
MLP_CHUNK = 128
ATT_Q_HEADS = 6
ATT_KV_HEADS = 2
ATT_GROUP = ATT_Q_HEADS // ATT_KV_HEADS
ATT_WIDTH = ATT_Q_HEADS * HEAD_DIM
KV_WIDTH = ATT_KV_HEADS * HEAD_DIM
ROPE_THETA = 10000.0
DN_HEADS = 6
DN_WIDTH = DN_HEADS * HEAD_DIM
DN_CONV = 5
DN_CHUNK = 64
A_COLS = 2 * A_WIDTH
B_COLS = ATT_WIDTH + 2 * KV_WIDTH
QKV_COLS = 3 * DN_WIDTH
GATE_COLS = 4 * DN_HEADS
IN_COLS = A_COLS + B_COLS + QKV_COLS + DN_WIDTH + GATE_COLS
LANES = 128
SUBLANES = 8
IN_COLS_PAD = IN_COLS - GATE_COLS + LANES
EPS = 1e-6
MOD_ROWS = 8
VMEM_LIMIT = 56 * 1024 * 1024


def _dot(a, b):
    return jnp.dot(a, b, preferred_element_type=F32)


def _dot_nt(a, b):
    return lax.dot_general(a, b, (((1,), (1,)), ((), ())), preferred_element_type=F32)


def _dot_tn(a, b):
    return lax.dot_general(a, b, (((0,), (0,)), ((), ())), preferred_element_type=F32)


def _split(x):
    hi = x.astype(BF16)
    lo = (x - hi.astype(F32)).astype(BF16)
    return hi, lo


def _dot_x3(a, b):
    ah, al = _split(a)
    bh, bl = _split(b)
    return _dot(ah, bh) + (_dot(ah, bl) + _dot(al, bh))


def _dot_lhs_split(x, m):
    hi, lo = _split(x)
    return _dot(hi, m) + _dot(lo, m)


def _norm(x):
    mu = jnp.mean(x, -1, keepdims=True)
    xc = x - mu
    var = jnp.mean(xc * xc, -1, keepdims=True)
    return xc * lax.rsqrt(var + EPS)


def _sigmoid(x):
    return 1.0 / (1.0 + jnp.exp(-x))


def _silu(x):
    return x * _sigmoid(x)


def _params(n_grid, vmem=VMEM_LIMIT):
    return pltpu.CompilerParams(dimension_semantics=("arbitrary",) * n_grid, vmem_limit_bytes=vmem)


def _const_spec(shape):
    nd = len(shape)
    return pl.BlockSpec(shape, lambda *_: (0,) * nd)


def _mod_kernel(cs_ref, w_ref, b_ref, o_ref):
    cs = cs_ref[...]
    o_ref[0] = _dot_x3(_silu(cs), w_ref[0]) + b_ref[0]


def _modulation(cs, mod_w, mod_b):
    depth, d, n = mod_w.shape
    tn = 1536
    return pl.pallas_call(
        _mod_kernel,
        grid=(depth, n // tn),
        in_specs=[
            pl.BlockSpec((MOD_ROWS, d), lambda l, j: (0, 0)),
            pl.BlockSpec((1, d, tn), lambda l, j: (l, 0, j)),
            pl.BlockSpec((1, 1, tn), lambda l, j: (l, 0, j)),
        ],
        out_specs=pl.BlockSpec((1, MOD_ROWS, tn), lambda l, j: (l, 0, j)),
        out_shape=jax.ShapeDtypeStruct((depth, MOD_ROWS, n), F32),
        compiler_params=_params(2),
        name="modulation",
    )(cs, mod_w, mod_b.reshape(depth, 1, n))


def _gmlp_rows(pa, lng_ref, lnb_ref, ws_ref, bias_ref, o_ref):
    for ci in range(pa.shape[0] // MLP_CHUNK):
        rows = slice(ci * MLP_CHUNK, (ci + 1) * MLP_CHUNK)
        p = pa[rows]
        a = 0.5 * p * (1.0 + lax.erf(p * (2.0 ** -0.5)))
        u = a[:, :A_WIDTH]
        v = _norm(a[:, A_WIDTH:]) * lng_ref[...] + lnb_ref[...]
        vb = v.astype(BF16)
        mixed = jnp.concatenate(
            [_dot(ws_ref[g], vb[:, g * HEAD_DIM:(g + 1) * HEAD_DIM]) for g in range(A_GROUPS)], axis=-1)
        o_ref[0, rows, :] = (u * (mixed + bias_ref[...])).astype(BF16)


def _swap_rope_pairs(x):
    n = x.shape[-1]
    lane = lax.broadcasted_iota(jnp.int32, x.shape, x.ndim - 1)
    first = (lane & 31) < 16
    return jnp.where(first, pltpu.roll(x, n - 16, x.ndim - 1), pltpu.roll(x, 16, x.ndim - 1))


def _head_rsqrt(x, red_ref, exp_ref):
    s = _dot_lhs_split(x * x, red_ref[...])
    return _dot_lhs_split(lax.rsqrt(s + EPS), exp_ref[...])


def _head_maps(width, value):
    head = jnp.arange(width) // HEAD_DIM
    col = jnp.arange(LANES)
    red = jnp.where(head[:, None] == col[None, :], value, 0.0).astype(BF16)
    return red, (col[:, None] == head[None, :]).astype(BF16)


def _attn_prep_rows(pb, cos_ref, sin_ref, gain_ref, red_ref, exp_ref, q_out, k_out, v_out):
    nqk = ATT_WIDTH + KV_WIDTH
    qk = pb[:, :nqk]
    qk = qk * _head_rsqrt(qk, red_ref, exp_ref) * gain_ref[...]
    r = qk * cos_ref[...] + _swap_rope_pairs(qk) * sin_ref[...]
    for h in range(ATT_Q_HEADS):
        q_out[0, h] = r[:, h * HEAD_DIM:(h + 1) * HEAD_DIM].astype(BF16)
    vv = pb[:, nqk:nqk + KV_WIDTH]
    lane = lax.broadcasted_iota(jnp.int32, vv.shape, 1)
    for h in range(ATT_KV_HEADS):
        k_out[0, h] = r[:, ATT_WIDTH + h * HEAD_DIM:ATT_WIDTH + (h + 1) * HEAD_DIM].astype(BF16)
        vh = vv if h == 0 else pltpu.roll(vv, KV_WIDTH - h * HEAD_DIM, 1)
        v_out[0, h] = jnp.where(lane < HEAD_DIM, vh, jnp.where(lane == HEAD_DIM, 1.0, 0.0)).astype(BF16)


KV_BLOCK = 256


def _key_blocks(nkeys):
    return [(s0, min(KV_BLOCK, nkeys - s0)) for s0 in range(0, nkeys, KV_BLOCK)]


def _store_heads(o_ref, acc, tq):
    o = acc[:, :HEAD_DIM] / acc[:, HEAD_DIM:HEAD_DIM + 1]
    for g in range(ATT_GROUP):
        o_ref[0, 0, :, g * HEAD_DIM:(g + 1) * HEAD_DIM] = o[g * tq:(g + 1) * tq].astype(BF16)


def _attn_lat_kernel(q_ref, k_ref, v_ref, o_ref, s_ref, mx_ref, *, tq):
    i = pl.program_id(2)
    cur = i % 2
    prev = 1 - cur
    t = k_ref.shape[2]

    @pl.when(i == 0)
    def _():
        s_ref[1] = jnp.zeros(s_ref.shape[1:], F32)
        mx_ref[1] = jnp.zeros(mx_ref.shape[1:], F32)

    q = q_ref[0].reshape(ATT_GROUP * tq, HEAD_DIM)
    m = jnp.max(mx_ref[prev], -1, keepdims=True)
    mx = None
    acc = None
    for s0, n in _key_blocks(t):
        s = _dot_nt(q, k_ref[0, 0, s0:s0 + n, :])
        s_ref[cur, :, s0:s0 + n] = s
        for l0 in range(0, n, LANES):
            part = s[:, l0:l0 + LANES]
            mx = part if mx is None else jnp.maximum(mx, part)
        p = jnp.exp2(s_ref[prev, :, s0:s0 + n] - m).astype(BF16)
        part = _dot(p, v_ref[0, 0, s0:s0 + n, :])
        acc = part if acc is None else acc + part
    mx_ref[cur] = mx
    _store_heads(o_ref, acc, tq)


def _attn_ctx_kernel(q_ref, k_ref, v_ref, o_ref, *, tc):
    q = q_ref[0].reshape(ATT_GROUP * tc, HEAD_DIM)
    s = _dot_nt(q, k_ref[0, 0])
    p = jnp.exp2(s - jnp.max(s, -1, keepdims=True)).astype(BF16)
    _store_heads(o_ref, _dot(p, v_ref[0, 0]), tc)


def _attention(q, k, v, tc, need_ctx, tq=128):
    bsz, _, t, _ = q.shape
    gw = ATT_GROUP * HEAD_DIM
    first = tc // tq
    nq = (t - tc) // tq
    y_lat = pl.pallas_call(
        functools.partial(_attn_lat_kernel, tq=tq),
        grid=(bsz, ATT_KV_HEADS, nq + 1),
        in_specs=[
            pl.BlockSpec((1, ATT_GROUP, tq, HEAD_DIM), lambda b, j, i: (b, j, first + jnp.minimum(i, nq - 1), 0)),
            pl.BlockSpec((1, 1, t, HEAD_DIM), lambda b, j, i: (b, j, 0, 0)),
            pl.BlockSpec((1, 1, t, LANES), lambda b, j, i: (b, j, 0, 0)),
        ],
        out_specs=pl.BlockSpec((1, 1, tq, gw), lambda b, j, i: (b, j, jnp.maximum(i - 1, 0), 0)),
        out_shape=jax.ShapeDtypeStruct((bsz, ATT_KV_HEADS, t - tc, gw), BF16),
        scratch_shapes=[pltpu.VMEM((2, ATT_GROUP * tq, t), F32), pltpu.VMEM((2, ATT_GROUP * tq, LANES), F32)],
        compiler_params=_params(3),
        name="attention",
    )(q, k, v)
    if not need_ctx:
        return None, y_lat
    y_ctx = pl.pallas_call(
        functools.partial(_attn_ctx_kernel, tc=tc),
        grid=(bsz, ATT_KV_HEADS),
        in_specs=[
            pl.BlockSpec((1, ATT_GROUP, tc, HEAD_DIM), lambda b, j: (b, j, 0, 0)),
            pl.BlockSpec((1, 1, tc, HEAD_DIM), lambda b, j: (b, j, 0, 0)),
            pl.BlockSpec((1, 1, tc, LANES), lambda b, j: (b, j, 0, 0)),
        ],
        out_specs=pl.BlockSpec((1, 1, tc, gw), lambda b, j: (b, j, 0, 0)),
        out_shape=jax.ShapeDtypeStruct((bsz, ATT_KV_HEADS, tc, gw), BF16),
        compiler_params=_params(2),
        name="attention_ctx",
    )(q, k, v)
    return y_ctx, y_lat


HALO = SUBLANES


def _dn_prep_rows(xe_ref, pg, cw_ref, red_ref, exp_ref, al_ref, dt_ref, q_out, k_out, v_out, g_out, tm):
    first = HALO - DN_CONV // 2
    acc = cw_ref[0:1, :] * xe_ref[first:first + tm, :]
    for j in range(1, DN_CONV):
        acc = acc + cw_ref[j:j + 1, :] * xe_ref[first + j:first + j + tm, :]
    y = _silu(acc)
    qk = y[:, :2 * DN_WIDTH]
    qk = qk * _head_rsqrt(qk, red_ref, exp_ref)
    q_out[0] = qk[:, :DN_WIDTH] * (HEAD_DIM ** -0.5)
    k_out[0] = qk[:, DN_WIDTH:]
    v_out[0] = y[:, 2 * DN_WIDTH:]
    z = pg + dt_ref[...]
    softplus = jnp.maximum(z, 0.0) + jnp.log1p(jnp.exp(-jnp.abs(z)))
    lane = lax.broadcasted_iota(jnp.int32, pg.shape, 1)
    g_out[0] = jnp.where(lane < 2 * DN_HEADS, -jnp.exp(al_ref[...]) * softplus, _sigmoid(pg))


def _pre_mixer_kernel(x_ref, xp_ref, xn_ref, sh_ref, sc_ref, w_ref,
                      lng_ref, lnb_ref, ws_ref, bias_ref,
                      cos_ref, sin_ref, again_ref, ared_ref, aexp_ref,
                      cw_ref, dred_ref, dexp_ref, al_ref, dt_ref,
                      ya_out, q_out, k_out, v_out, dq_out, dk_out, dv_out, g_out, z_out,
                      xe_ref, *, tm, ncb, nblk):
    i = pl.program_id(1)
    seg_start = jnp.logical_or(i == 0, i == ncb)
    seg_end = jnp.logical_or(i == ncb - 1, i == nblk - 1)
    rows = jnp.concatenate([xp_ref[0], x_ref[0], xn_ref[0]], axis=0)
    hf = _norm(rows) * (1.0 + sc_ref[...]) + sh_ref[...]
    h = hf[HALO:HALO + tm].astype(BF16)
    o_b, o_q, o_z, o_g = A_COLS, A_COLS + B_COLS, A_COLS + B_COLS + QKV_COLS, IN_COLS - GATE_COLS
    xe_ref[...] = _dot(hf.astype(BF16), w_ref[:, o_q:o_z])
    xe_ref[0:HALO, :] = jnp.where(seg_start, 0.0, xe_ref[0:HALO, :])
    xe_ref[HALO + tm:2 * HALO + tm, :] = jnp.where(seg_end, 0.0, xe_ref[HALO + tm:2 * HALO + tm, :])
    _gmlp_rows(_dot(h, w_ref[:, :o_b]), lng_ref, lnb_ref, ws_ref, bias_ref, ya_out)
    _attn_prep_rows(_dot(h, w_ref[:, o_b:o_q]), cos_ref, sin_ref, again_ref, ared_ref, aexp_ref,
                    q_out, k_out, v_out)
    z_out[0] = _dot(h, w_ref[:, o_z:o_g]).astype(BF16)
    _dn_prep_rows(xe_ref, _dot(h, w_ref[:, o_g:]), cw_ref, dred_ref, dexp_ref, al_ref, dt_ref,
                  dq_out, dk_out, dv_out, g_out, tm)


def _pre_mixer(xcat, modl, w_in_p, ln_g, ln_b, w_s, b_s, cos_t, sin_t, q_g, k_g, conv_w, a_log, dt_bias,
               tm, ncb, ctx_row):
    bsz, t, d = xcat.shape
    nblk = t // tm
    hb = tm // HALO
    nqk = ATT_WIDTH + KV_WIDTH
    bias = jnp.repeat(b_s.T, HEAD_DIM, axis=1)
    gain = jnp.concatenate([jnp.tile(q_g, ATT_Q_HEADS), jnp.tile(k_g, ATT_KV_HEADS)]).reshape(1, nqk)
    ared, aexp = _head_maps(nqk, 1.0 / HEAD_DIM)
    dred, dexp = _head_maps(2 * DN_WIDTH, 1.0)
    pad = lambda v: jnp.zeros((1, LANES), F32).at[0, :2 * DN_HEADS].set(v.reshape(-1))
    mspec = lambda c: pl.BlockSpec((None, None, 1, d), lambda b, i: (jnp.where(i < ncb, ctx_row, b), c, 0, 0))
    rspec = lambda w: pl.BlockSpec((1, tm, w), lambda b, i: (b, i, 0))
    hspec = lambda nh, w: pl.BlockSpec((1, nh, tm, w), lambda b, i: (b, 0, i, 0))
    heads = ((ATT_Q_HEADS, HEAD_DIM), (ATT_KV_HEADS, HEAD_DIM), (ATT_KV_HEADS, LANES))
    rows_out = ((DN_WIDTH, F32), (DN_WIDTH, F32), (DN_WIDTH, F32), (LANES, F32), (DN_WIDTH, BF16))
    return pl.pallas_call(
        functools.partial(_pre_mixer_kernel, tm=tm, ncb=ncb, nblk=nblk),
        grid=(bsz, nblk),
        in_specs=[
            rspec(d),
            pl.BlockSpec((1, HALO, d), lambda b, i: (b, jnp.maximum(i * hb - 1, 0), 0)),
            pl.BlockSpec((1, HALO, d), lambda b, i: (b, jnp.minimum((i + 1) * hb, nblk * hb - 1), 0)),
            mspec(0), mspec(1),
            pl.BlockSpec(w_in_p.shape, lambda b, i: (0, 0), pipeline_mode=pl.Buffered(1)),
            _const_spec((1, A_WIDTH)), _const_spec((1, A_WIDTH)), _const_spec(w_s.shape), _const_spec(bias.shape),
            pl.BlockSpec((tm, nqk), lambda b, i: (i, 0)), pl.BlockSpec((tm, nqk), lambda b, i: (i, 0)),
            _const_spec((1, nqk)), _const_spec(ared.shape), _const_spec(aexp.shape),
            _const_spec(conv_w.shape), _const_spec(dred.shape), _const_spec(dexp.shape),
            _const_spec((1, LANES)), _const_spec((1, LANES)),
        ],
        out_specs=[rspec(A_WIDTH)] + [hspec(nh, w) for nh, w in heads] + [rspec(w) for w, _ in rows_out],
        out_shape=[jax.ShapeDtypeStruct((bsz, t, A_WIDTH), BF16)]
        + [jax.ShapeDtypeStruct((bsz, nh, t, w), BF16) for nh, w in heads]
        + [jax.ShapeDtypeStruct((bsz, t, w), dt) for w, dt in rows_out],
        scratch_shapes=[pltpu.VMEM((tm + 2 * HALO, QKV_COLS), F32)],
        compiler_params=_params(2),
        name="pre_mixer",
    )(xcat, xcat, xcat, modl, modl, w_in_p,
      ln_g.reshape(1, -1), ln_b.reshape(1, -1), w_s.astype(BF16), bias,
      cos_t, sin_t, gain, ared, aexp, conv_w, dred, dexp, pad(a_log), pad(dt_bias))


TILE_A = 4 * HEAD_DIM


def _split3(x):
    hi = x.astype(BF16)
    r = x - hi.astype(F32)
    mid = r.astype(BF16)
    lo = (r - mid.astype(F32)).astype(BF16)
    return hi, mid, lo


def _head_id(shape, axis):
    return jnp.right_shift(lax.broadcasted_iota(jnp.int32, shape, axis), 6)


class _Packed:
    def __init__(self):
        c, w = DN_CHUNK, DN_WIDTH
        wb = w - TILE_A
        self.mask_a = _head_id((TILE_A, TILE_A), 0) == _head_id((TILE_A, TILE_A), 1)
        self.mask_b = _head_id((wb, wb), 0) == _head_id((wb, wb), 1)
        self.ri = lax.broadcasted_iota(jnp.int32, (c, w), 0)
        self.cj = lax.broadcasted_iota(jnp.int32, (c, w), 1) & (HEAD_DIM - 1)
        self.eye = (self.ri == self.cj).astype(F32)
        self.incl = (self.cj <= self.ri, self.cj >= self.ri)
        self.strict = (self.cj < self.ri, self.cj > self.ri)
        r64 = lax.broadcasted_iota(jnp.int32, (c, c), 0)
        c64 = lax.broadcasted_iota(jnp.int32, (c, c), 1)
        self.order = ((c64 <= r64).astype(BF16), (c64 >= r64).astype(BF16))
        self.blk = {b: jnp.right_shift(self.ri, int(math.log2(b))) == jnp.right_shift(self.cj, int(math.log2(b)))
                    for b in (8, 16, 32)}

    def weights(self, y):
        y16 = y.astype(BF16)
        ya = jnp.concatenate([y16[:, :TILE_A]] * 4, axis=0)
        yb = jnp.concatenate([y16[:, TILE_A:]] * 2, axis=0)
        zero = jnp.zeros((), BF16)
        return jnp.where(self.mask_a, ya, zero), jnp.where(self.mask_b, yb, zero)

    def dot(self, x, y):
        wa, wb = self.weights(y)
        x16 = x.astype(BF16)
        return jnp.concatenate([_dot(x16[:, :TILE_A], wa), _dot(x16[:, TILE_A:], wb)], axis=-1)

    def dot_nt(self, x, y):
        wa, wb = self.weights(y)
        x16 = x.astype(BF16)
        return jnp.concatenate([_dot_nt(x16[:, :TILE_A], wa), _dot_nt(x16[:, TILE_A:], wb)], axis=-1)

    def dot_tn(self, x, y):
        x16, y16 = x.astype(BF16), y.astype(BF16)
        ta = _dot_tn(x16[:, :TILE_A], y16[:, :TILE_A])
        tb = _dot_tn(x16[:, TILE_A:], y16[:, TILE_A:])
        return jnp.where(self.mask_a, ta, 0.0), jnp.where(self.mask_b, tb, 0.0)

    def tri_inverse(self, lows):
        c = DN_CHUNK
        nb = [-jnp.where(self.blk[8], low, 0.0) for low in lows]
        x = [self.eye + a for a in nb]
        pw = [self.dot(a, a) for a in nb]
        st = [self.dot(jnp.concatenate([xa, p], axis=0), p) for xa, p in zip(x, pw)]
        x = [xa + s[:c] for xa, s in zip(x, st)]
        x = [xa + self.dot(xa, s[c:]) for xa, s in zip(x, st)]
        for b in (8, 16, 32):
            inner = self.blk[b]
            outer = self.blk[2 * b] if 2 * b in self.blk else None
            off = jnp.logical_not(inner) if outer is None else jnp.logical_and(outer, jnp.logical_not(inner))
            xc = [self.dot(xa, jnp.where(off, low, 0.0)) for xa, low in zip(x, lows)]
            x = [xa - self.dot(t, xa) for xa, t in zip(x, xc)]
        return x


def _spread_heads(x, first):
    c = x.shape[0]
    low_half = lax.broadcasted_iota(jnp.int32, (c, LANES), 1) < HEAD_DIM
    tiles = []
    for j in range(0, DN_HEADS, 2):
        a = jnp.broadcast_to(x[:, first + j:first + j + 1], (c, LANES))
        b = jnp.broadcast_to(x[:, first + j + 1:first + j + 2], (c, LANES))
        tiles.append(jnp.where(low_half, a, b))
    return jnp.concatenate(tiles, axis=-1)


def _dn_chunk_terms(pk, dirs, q, k, v, gates):
    c = DN_CHUNK
    incl = [pk.incl[d] for d in dirs]
    strict = [pk.strict[d] for d in dirs]
    gc = [sum(_dot(pk.order[d], p) for p in _split3(g)) for d, g in zip(dirs, gates)]
    gcx = [_spread_heads(a, d * DN_HEADS) for d, a in zip(dirs, gc)]
    bx = [_spread_heads(g, (2 + d) * DN_HEADS) for d, g in zip(dirs, gates)]
    gtot = [a[c - 1:c, :] if d == 0 else a[0:1, :] for d, a in zip(dirs, gcx)]
    grow = [jnp.sum(pk.eye * a, 0, keepdims=True) for a in gcx]
    decay = [jnp.where(m, jnp.exp(jnp.minimum(a - r, 0.0)), 0.0) for m, a, r in zip(incl, gcx, grow)]
    kbeta = [a * b for a, b in zip(k, bx)]
    kq = [pk.dot_nt(jnp.concatenate([a, b], axis=0), kk) for a, b, kk in zip(kbeta, q, k)]
    low = [jnp.where(m, a[:c] * dc, 0.0) for m, a, dc in zip(strict, kq, decay)]
    intra = [jnp.where(m, a[c:] * dc, 0.0) for m, a, dc in zip(incl, kq, decay)]
    tinv = pk.tri_inverse(low)
    eg = [jnp.exp(a) for a in gcx]
    u = [pk.dot(t, a * b) for t, a, b in zip(tinv, v, bx)]
    wm = [pk.dot(t, a * e) for t, a, e in zip(tinv, kbeta, eg)]
    lhs = [jnp.concatenate([a * e, w_], axis=0).astype(BF16) for a, e, w_ in zip(q, eg, wm)]
    k_dec = [a * jnp.exp(gt - gx) for a, gt, gx in zip(k, gtot, gcx)]
    egl = [jnp.exp(gt) for gt in gtot]
    return [dict(lhs=a, u=b, intra=i, k_dec=kd, egl=e) for a, b, i, kd, e in zip(lhs, u, intra, k_dec, egl)]


def _dn_scan_kernel(qf, kf, vf, gf, qb, kb_, vb, gb, of, ob, sa_ref, sb_ref, *, nsub):
    c = DN_CHUNK

    @pl.when(pl.program_id(1) == 0)
    def _():
        sa_ref[...] = jnp.zeros_like(sa_ref)
        sb_ref[...] = jnp.zeros_like(sb_ref)

    pk = _Packed()
    refs = ((qf, kf, vf, gf, of), (qb, kb_, vb, gb, ob))
    items = [(d, j if d == 0 else nsub - 1 - j) for j in range(nsub) for d in (0, 1)]
    rows = [slice(j * c, (j + 1) * c) for _, j in items]
    dirs = [d for d, _ in items]
    terms = _dn_chunk_terms(
        pk, dirs,
        [refs[d][0][0, r, :] for d, r in zip(dirs, rows)],
        [refs[d][1][0, r, :] for d, r in zip(dirs, rows)],
        [refs[d][2][0, r, :] for d, r in zip(dirs, rows)],
        [refs[d][3][0, r, :] for d, r in zip(dirs, rows)])
    state = [(sa_ref[d], sb_ref[d]) for d in (0, 1)]
    for d, r, t in zip(dirs, rows, terms):
        sa, sb = state[d]
        lhs = t["lhs"]
        res = jnp.concatenate([_dot(lhs[:, :TILE_A], sa.astype(BF16)), _dot(lhs[:, TILE_A:], sb.astype(BF16))],
                              axis=-1)
        v_new = t["u"] - res[c:]
        refs[d][4][0, r, :] = res[:c] + pk.dot(t["intra"], v_new)
        ta, tb = pk.dot_tn(t["k_dec"], v_new)
        state[d] = (sa * t["egl"][:, :TILE_A] + ta, sb * t["egl"][:, TILE_A:] + tb)
    for d in (0, 1):
        sa_ref[d], sb_ref[d] = state[d]


DN_BLOCK = 2 * DN_CHUNK


def _dn_scan(q, k, v, g, tc):
    bsz, t, w = q.shape
    c = DN_BLOCK
    nct, ncx = t // c, tc // c

    def bwd_blk(s):
        return jnp.where(s < ncx, ncx - 1 - s, nct - 1 - s + ncx)

    fspec = lambda width: pl.BlockSpec((1, c, width), lambda b, s: (b, s, 0))
    bspec = lambda width: pl.BlockSpec((1, c, width), lambda b, s: (b, bwd_blk(s), 0))
    return pl.pallas_call(
        functools.partial(_dn_scan_kernel, nsub=DN_BLOCK // DN_CHUNK),
        grid=(bsz, nct),
        in_specs=[fspec(w), fspec(w), fspec(w), fspec(LANES), bspec(w), bspec(w), bspec(w), bspec(LANES)],
        out_specs=[fspec(w), bspec(w)],
        out_shape=[jax.ShapeDtypeStruct((bsz, t, w), F32)] * 2,
        scratch_shapes=[pltpu.VMEM((2, TILE_A, TILE_A), F32), pltpu.VMEM((2, w - TILE_A, w - TILE_A), F32)],
        compiler_params=_params(2),
        name="dn_scan",
    )(q, k, v, g, q, k, v, g)


def _post_kernel(x_ref, ya_ref, yb0_ref, yb1_ref, of_ref, ob_ref, z_ref, ngain_ref, nred_ref, nexp_ref,
                 g1_ref, sh2_ref, sc2_ref, g2_ref,
                 wa_ref, wb0_ref, wb1_ref, wc_ref, l1g_ref, l1b_ref, l2g_ref, l2b_ref,
                 wup_ref, wdn_ref, o_ref, *, alpha, f_chunk):
    o = of_ref[0] + ob_ref[0]
    yc = (o * _head_rsqrt(o, nred_ref, nexp_ref) * ngain_ref[...] * _silu(z_ref[0].astype(F32))).astype(BF16)
    branch = (_dot(ya_ref[0], wa_ref[...]) + _dot(yb0_ref[0, 0], wb0_ref[...])
              + _dot(yb1_ref[0, 0], wb1_ref[...]) + _dot(yc, wc_ref[...]))
    x1 = _norm(alpha * x_ref[0] + g1_ref[...] * branch) * l1g_ref[...] + l1b_ref[...]
    h = (_norm(x1) * (1.0 + sc2_ref[...]) + sh2_ref[...]).astype(BF16)
    d_ff = wup_ref.shape[1]
    m = None
    for f in range(0, d_ff, f_chunk):
        up = jnp.maximum(_dot(h, wup_ref[:, f:f + f_chunk]), 0.0)
        part = _dot((up * up).astype(BF16), wdn_ref[f:f + f_chunk, :])
        m = part if m is None else m + part
    o_ref[0] = _norm(alpha * x1 + g2_ref[...] * m) * l2g_ref[...] + l2b_ref[...]


def _post(xcat, ya, yb, o_f, o_b, pz, norm_g, modl, w_out, ln1_g, ln1_b, ln2_g, ln2_b, w_up, w_down,
          tm, ncb, ctx_row, alpha, need_ctx):
    bsz, t, d = xcat.shape
    ngain = jnp.tile(norm_g, DN_HEADS).reshape(1, DN_WIDTH)
    nred, nexp = _head_maps(DN_WIDTH, 1.0 / HEAD_DIM)
    first = 0 if need_ctx else ncb
    yb_first = first - (t - yb.shape[2]) // tm
    gw = ATT_GROUP * HEAD_DIM
    o1 = A_WIDTH
    wa, wb0, wb1, wc = w_out[:o1], w_out[o1:o1 + gw], w_out[o1 + gw:o1 + 2 * gw], w_out[o1 + 2 * gw:]
    mspec = lambda c: pl.BlockSpec(
        (None, None, 1, d), lambda b, i: (jnp.where(i + first < ncb, ctx_row, b), c, 0, 0))
    rspec = lambda w: pl.BlockSpec((1, tm, w), lambda b, i: (b, i + first, 0))
    vec = lambda a: a.reshape(1, d)
    wspec = lambda a: pl.BlockSpec(a.shape, lambda b, i: (0, 0), pipeline_mode=pl.Buffered(1))
    weights = [w.astype(BF16) for w in (wa, wb0, wb1, wc)]
    w_up, w_down = w_up.astype(BF16), w_down.astype(BF16)
    return pl.pallas_call(
        functools.partial(_post_kernel, alpha=alpha, f_chunk=1024),
        grid=(bsz, t // tm - first),
        in_specs=[
            rspec(d), rspec(A_WIDTH),
            pl.BlockSpec((1, 1, tm, gw), lambda b, i: (b, 0, i + yb_first, 0)),
            pl.BlockSpec((1, 1, tm, gw), lambda b, i: (b, 1, i + yb_first, 0)),
            rspec(DN_WIDTH), rspec(DN_WIDTH), rspec(DN_WIDTH),
            _const_spec(ngain.shape), _const_spec(nred.shape), _const_spec(nexp.shape),
            mspec(2), mspec(3), mspec(4), mspec(5),
            *[wspec(w) for w in weights],
            _const_spec((1, d)), _const_spec((1, d)), _const_spec((1, d)), _const_spec((1, d)),
            wspec(w_up), wspec(w_down),
        ],
        out_specs=pl.BlockSpec((1, tm, d), lambda b, i: (b, i, 0)),
        out_shape=jax.ShapeDtypeStruct((bsz, t - first * tm, d), F32),
        compiler_params=_params(2),
        name="post_mixer",
    )(xcat, ya, yb, yb, o_f, o_b, pz, ngain, nred, nexp, modl, modl, modl, modl, *weights,
      vec(ln1_g), vec(ln1_b), vec(ln2_g), vec(ln2_b), w_up, w_down)


def _rope_tables(tc, tl):
    pos = jnp.arange(tl)
    row = (pos // GRID_W).astype(F32)
    col = (pos % GRID_W).astype(F32)
    half = HEAD_DIM // 2
    inv = 1.0 / (ROPE_THETA ** (jnp.arange(0, half, 2, dtype=F32) / half))
    ar, ac = row[:, None] * inv, col[:, None] * inv
    cos = jnp.concatenate([jnp.cos(ar), jnp.cos(ar), jnp.cos(ac), jnp.cos(ac)], -1)
    sin = jnp.concatenate([-jnp.sin(ar), jnp.sin(ar), -jnp.sin(ac), jnp.sin(ac)], -1)
    cos = jnp.concatenate([jnp.ones((tc, HEAD_DIM), F32), cos], 0)
    sin = jnp.concatenate([jnp.zeros((tc, HEAD_DIM), F32), sin], 0)
    scale = HEAD_DIM ** -0.5 * math.log2(math.e)
    tile =lambda a: jnp.concatenate([jnp.tile(a, (1, ATT_Q_HEADS)) * scale, jnp.tile(a, (1, ATT_KV_HEADS))], -1)
    return tile(cos), tile(sin)


def kernel(x, c, ctx, c_ctx, mod_w, mod_b, w_in, w_out, gmlp_ln_g, gmlp_ln_b, gmlp_w_s, gmlp_b_s,
           attn_q_g, attn_k_g, dn_conv_w, dn_a_log, dn_dt_bias, dn_norm_g,
           ln1_g, ln1_b, ln2_g, ln2_b, w_up, w_down):
    bsz, tl, d = x.shape
    tc = ctx.shape[1]
    depth = mod_w.shape[0]
    assert bsz < MOD_ROWS and tl % GRID_W == 0 and tc % MLP_CHUNK == 0 and tl % MLP_CHUNK == 0
    tm = 256 if tc % 256 == 0 else MLP_CHUNK
    ncb = tc // tm
    ctx_row = bsz
    alpha = (2 * depth) ** 0.25

    cs = jnp.zeros((MOD_ROWS, d), F32).at[:bsz].set(c).at[ctx_row].set(c_ctx)
    mod = _modulation(cs, mod_w, mod_b).reshape(depth, MOD_ROWS, 6, 1, d)
    cos_t, sin_t = _rope_tables(tc, tl)
    xcat = jnp.concatenate([ctx, x], axis=1)

    for l in range(depth):
        need_ctx = l < depth - 1
        w_in_p = jnp.pad(w_in[l], ((0, 0), (0, IN_COLS_PAD - IN_COLS))).astype(BF16)
        ya, qa, ka, va, dq, dk, dv, dg, pz = _pre_mixer(
            xcat, mod[l], w_in_p, gmlp_ln_g[l], gmlp_ln_b[l], gmlp_w_s[l], gmlp_b_s[l], cos_t, sin_t,
            attn_q_g[l], attn_k_g[l], dn_conv_w[l], dn_a_log[l], dn_dt_bias[l], tm, ncb, ctx_row)
        yb_ctx, yb = _attention(qa, ka, va, tc, need_ctx)
        if need_ctx:
            yb = jnp.concatenate([yb_ctx, yb], axis=2)
        o_f, o_b = _dn_scan(dq, dk, dv, dg, tc)
        xcat = _post(xcat, ya, yb, o_f, o_b, pz, dn_norm_g[l], mod[l], w_out[l], ln1_g[l], ln1_b[l], ln2_g[l], ln2_b[l],
                     w_up[l], w_down[l], tm, ncb, ctx_row, alpha, need_ctx)
    return xcat
```

```python
import functools
import math

import jax
import jax.numpy as jnp
from jax import lax
from jax.experimental import pallas as pl
from jax.experimental.pallas import tpu as pltpu

F32 = jnp.float32
BF16 = jnp.bfloat16

GRID_W = 64
HEAD_DIM = 64
A_GROUPS = 4
A_WIDTH = A_GROUPS * HEAD_DIM
MLP_CHUNK = 128
ATT_Q_HEADS = 6
ATT_KV_HEADS = 2
ATT_GROUP = ATT_Q_HEADS // ATT_KV_HEADS
ATT_WIDTH = ATT_Q_HEADS * HEAD_DIM
KV_WIDTH = ATT_KV_HEADS * HEAD_DIM
ROPE_THETA = 10000.0
DN_HEADS = 6
DN_WIDTH = DN_HEADS * HEAD_DIM
DN_CONV = 5
DN_CHUNK = 64
A_COLS = 2 * A_WIDTH
B_COLS = ATT_WIDTH + 2 * KV_WIDTH
QKV_COLS = 3 * DN_WIDTH
GATE_COLS = 4 * DN_HEADS
IN_COLS = A_COLS + B_COLS + QKV_COLS + DN_WIDTH + GATE_COLS
LANES = 128
SUBLANES = 8
IN_COLS_PAD = IN_COLS - GATE_COLS + LANES
EPS = 1e-6
MOD_ROWS = 8
VMEM_LIMIT = 56 * 1024 * 1024


def _dot(a, b):
    return jnp.dot(a, b, preferred_element_type=F32)


def _dot_nt(a, b):
    return lax.dot_general(a, b, (((1,), (1,)), ((), ())), preferred_element_type=F32)


def _dot_tn(a, b):
    return lax.dot_general(a, b, (((0,), (0,)), ((), ())), preferred_element_type=F32)


def _split(x):
    hi = x.astype(BF16)
    lo = (x - hi.astype(F32)).astype(BF16)
    return hi, lo


def _dot_x3(a, b):
    ah, al = _split(a)
    bh, bl = _split(b)
    return _dot(ah, bh) + (_dot(ah, bl) + _dot(al, bh))


def _dot_lhs_split(x, m):
    hi, lo = _split(x)
    return _dot(hi, m) + _dot(lo, m)


def _norm(x):
    mu = jnp.mean(x, -1, keepdims=True)
    xc = x - mu
    var = jnp.mean(xc * xc, -1, keepdims=True)
    return xc * lax.rsqrt(var + EPS)


def _sigmoid(x):
    return 1.0 / (1.0 + jnp.exp(-x))


def _silu(x):
    return x * _sigmoid(x)


def _params(n_grid, vmem=VMEM_LIMIT):
    return pltpu.CompilerParams(dimension_semantics=("arbitrary",) * n_grid, vmem_limit_bytes=vmem)


def _const_spec(shape):
    nd = len(shape)
    return pl.BlockSpec(shape, lambda *_: (0,) * nd)


def _mod_kernel(cs_ref, w_ref, b_ref, o_ref):
    cs = cs_ref[...]
    o_ref[0] = _dot_x3(_silu(cs), w_ref[0]) + b_ref[0]


def _modulation(cs, mod_w, mod_b):
    depth, d, n = mod_w.shape
    tn = 1536
    return pl.pallas_call(
        _mod_kernel,
        grid=(depth, n // tn),
        in_specs=[
            pl.BlockSpec((MOD_ROWS, d), lambda l, j: (0, 0)),
            pl.BlockSpec((1, d, tn), lambda l, j: (l, 0, j)),
            pl.BlockSpec((1, 1, tn), lambda l, j: (l, 0, j)),
        ],
        out_specs=pl.BlockSpec((1, MOD_ROWS, tn), lambda l, j: (l, 0, j)),
        out_shape=jax.ShapeDtypeStruct((depth, MOD_ROWS, n), F32),
        compiler_params=_params(2),
        name="modulation",
    )(cs, mod_w, mod_b.reshape(depth, 1, n))


def _gmlp_rows(pa, lng_ref, lnb_ref, ws_ref, bias_ref, o_ref):
    for ci in range(pa.shape[0] // MLP_CHUNK):
        rows = slice(ci * MLP_CHUNK, (ci + 1) * MLP_CHUNK)
        p = pa[rows]
        a = 0.5 * p * (1.0 + lax.erf(p * (2.0 ** -0.5)))
        u = a[:, :A_WIDTH]
        v = _norm(a[:, A_WIDTH:]) * lng_ref[...] + lnb_ref[...]
        vb = v.astype(BF16)
        mixed = jnp.concatenate(
            [_dot(ws_ref[g], vb[:, g * HEAD_DIM:(g + 1) * HEAD_DIM]) for g in range(A_GROUPS)], axis=-1)
        o_ref[0, rows, :] = (u * (mixed + bias_ref[...])).astype(BF16)


def _swap_rope_pairs(x):
    n = x.shape[-1]
    lane = lax.broadcasted_iota(jnp.int32, x.shape, x.ndim - 1)
    first = (lane & 31) < 16
    return jnp.where(first, pltpu.roll(x, n - 16, x.ndim - 1), pltpu.roll(x, 16, x.ndim - 1))


def _head_rsqrt(x, red_ref, exp_ref):
    s = _dot_lhs_split(x * x, red_ref[...])
    return _dot_lhs_split(lax.rsqrt(s + EPS), exp_ref[...])


def _head_maps(width, value):
    head = jnp.arange(width) // HEAD_DIM
    col = jnp.arange(LANES)
    red = jnp.where(head[:, None] == col[None, :], value, 0.0).astype(BF16)
    return red, (col[:, None] == head[None, :]).astype(BF16)


def _attn_prep_rows(pb, cos_ref, sin_ref, gain_ref, red_ref, exp_ref, q_out, k_out, v_out):
    nqk = ATT_WIDTH + KV_WIDTH
    qk = pb[:, :nqk]
    qk = qk * _head_rsqrt(qk, red_ref, exp_ref) * gain_ref[...]
    r = qk * cos_ref[...] + _swap_rope_pairs(qk) * sin_ref[...]
    for h in range(ATT_Q_HEADS):
        q_out[0, h] = r[:, h * HEAD_DIM:(h + 1) * HEAD_DIM].astype(BF16)
    vv = pb[:, nqk:nqk + KV_WIDTH]
    lane = lax.broadcasted_iota(jnp.int32, vv.shape, 1)
    for h in range(ATT_KV_HEADS):
        k_out[0, h] = r[:, ATT_WIDTH + h * HEAD_DIM:ATT_WIDTH + (h + 1) * HEAD_DIM].astype(BF16)
        vh = vv if h == 0 else pltpu.roll(vv, KV_WIDTH - h * HEAD_DIM, 1)
        v_out[0, h] = jnp.where(lane < HEAD_DIM, vh, jnp.where(lane == HEAD_DIM, 1.0, 0.0)).astype(BF16)


KV_BLOCK = 256


def _key_blocks(nkeys):
    return [(s0, min(KV_BLOCK, nkeys - s0)) for s0 in range(0, nkeys, KV_BLOCK)]


def _store_heads(o_ref, acc, tq):
    o = acc[:, :HEAD_DIM] / acc[:, HEAD_DIM:HEAD_DIM + 1]
    for g in range(ATT_GROUP):
        o_ref[0, 0, :, g * HEAD_DIM:(g + 1) * HEAD_DIM] = o[g * tq:(g + 1) * tq].astype(BF16)


def _attn_lat_kernel(q_ref, k_ref, v_ref, oe_ref, oo_ref, sa_ref, sb_ref, ma_ref, mb_ref, *, tq):
    t = k_ref.shape[2]

    @pl.when(pl.program_id(2) == 0)
    def _():
        sb_ref[...] = jnp.zeros(sb_ref.shape, F32)
        mb_ref[...] = jnp.zeros(mb_ref.shape, F32)

    def half(q, s_new, m_new, s_old, m_old, o_ref):
        m = jnp.max(m_old[...], -1, keepdims=True)
        mx = None
        acc = None
        for s0, n in _key_blocks(t):
            s = _dot_nt(q, k_ref[0, 0, s0:s0 + n, :])
            s_new[:, s0:s0 + n] = s
            for l0 in range(0, n, LANES):
                part = s[:, l0:l0 + LANES]
                mx = part if mx is None else jnp.maximum(mx, part)
            p = jnp.exp2(s_old[:, s0:s0 + n] - m).astype(BF16)
            part = _dot(p, v_ref[0, 0, s0:s0 + n, :])
            acc = part if acc is None else acc + part
        m_new[...] = mx
        _store_heads(o_ref, acc, tq)

    rows = ATT_GROUP * tq
    half(q_ref[0, :, :tq, :].reshape(rows, HEAD_DIM), sa_ref, ma_ref, sb_ref, mb_ref, oo_ref)
    half(q_ref[0, :, tq:, :].reshape(rows, HEAD_DIM), sb_ref, mb_ref, sa_ref, ma_ref, oe_ref)


def _attn_ctx_kernel(q_ref, k_ref, v_ref, o_ref, *, tc):
    q = q_ref[0].reshape(ATT_GROUP * tc, HEAD_DIM)
    s = _dot_nt(q, k_ref[0, 0])
    p = jnp.exp2(s - jnp.max(s, -1, keepdims=True)).astype(BF16)
    _store_heads(o_ref, _dot(p, v_ref[0, 0]), tc)


def _attention(q, k, v, tc, need_ctx, tq=128):
    bsz, _, t, _ = q.shape
    gw = ATT_GROUP * HEAD_DIM
    assert tc % (2 * tq) == 0 and (t - tc) % (2 * tq) == 0
    first = tc // (2 * tq)
    n2 = (t - tc) // (2 * tq)
    rows = ATT_GROUP * tq
    half_shape = jax.ShapeDtypeStruct((bsz, ATT_KV_HEADS, n2 * tq, gw), BF16)
    y_even, y_odd = pl.pallas_call(
        functools.partial(_attn_lat_kernel, tq=tq),
        grid=(bsz, ATT_KV_HEADS, n2 + 1),
        in_specs=[
            pl.BlockSpec((1, ATT_GROUP, 2 * tq, HEAD_DIM),
                         lambda b, j, i: (b, j, first + jnp.minimum(i, n2 - 1), 0)),
            pl.BlockSpec((1, 1, t, HEAD_DIM), lambda b, j, i: (b, j, 0, 0)),
            pl.BlockSpec((1, 1, t, LANES), lambda b, j, i: (b, j, 0, 0)),
        ],
        out_specs=[pl.BlockSpec((1, 1, tq, gw), lambda b, j, i: (b, j, jnp.minimum(i, n2 - 1), 0)),
                   pl.BlockSpec((1, 1, tq, gw), lambda b, j, i: (b, j, jnp.maximum(i - 1, 0), 0))],
        out_shape=[half_shape, half_shape],
        scratch_shapes=[pltpu.VMEM((rows, t), F32), pltpu.VMEM((rows, t), F32),
                        pltpu.VMEM((rows, LANES), F32), pltpu.VMEM((rows, LANES), F32)],
        compiler_params=_params(3),
        name="attention",
    )(q, k, v)
    y_lat = jnp.stack([y_even.reshape(bsz, ATT_KV_HEADS, n2, tq, gw),
                       y_odd.reshape(bsz, ATT_KV_HEADS, n2, tq, gw)], axis=3).reshape(bsz, ATT_KV_HEADS, t - tc, gw)
    if not need_ctx:
        return None, y_lat
    y_ctx = pl.pallas_call(
        functools.partial(_attn_ctx_kernel, tc=tc),
        grid=(bsz, ATT_KV_HEADS),
        in_specs=[
            pl.BlockSpec((1, ATT_GROUP, tc, HEAD_DIM), lambda b, j: (b, j, 0, 0)),
            pl.BlockSpec((1, 1, tc, HEAD_DIM), lambda b, j: (b, j, 0, 0)),
            pl.BlockSpec((1, 1, tc, LANES), lambda b, j: (b, j, 0, 0)),
        ],
        out_specs=pl.BlockSpec((1, 1, tc, gw), lambda b, j: (b, j, 0, 0)),
        out_shape=jax.ShapeDtypeStruct((bsz, ATT_KV_HEADS, tc, gw), BF16),
        compiler_params=_params(2),
        name="attention_ctx",
    )(q, k, v)
    return y_ctx, y_lat


HALO = SUBLANES


def _dn_prep_rows(xe_ref, pg, cw_ref, red_ref, exp_ref, al_ref, dt_ref, q_out, k_out, v_out, g_out, tm):
    first = HALO - DN_CONV // 2
    acc = cw_ref[0:1, :] * xe_ref[first:first + tm, :]
    for j in range(1, DN_CONV):
        acc = acc + cw_ref[j:j + 1, :] * xe_ref[first + j:first + j + tm, :]
    y = _silu(acc)
    qk = y[:, :2 * DN_WIDTH]
    qk = qk * _head_rsqrt(qk, red_ref, exp_ref)
    q_out[0] = qk[:, :DN_WIDTH] * (HEAD_DIM ** -0.5)
    k_out[0] = qk[:, DN_WIDTH:]
    v_out[0] = y[:, 2 * DN_WIDTH:]
    z = pg + dt_ref[...]
    softplus = jnp.maximum(z, 0.0) + jnp.log1p(jnp.exp(-jnp.abs(z)))
    lane = lax.broadcasted_iota(jnp.int32, pg.shape, 1)
    g_out[0] = jnp.where(lane < 2 * DN_HEADS, -jnp.exp(al_ref[...]) * softplus, _sigmoid(pg))


def _pre_mixer_kernel(x_ref, xp_ref, xn_ref, sh_ref, sc_ref, w_ref,
                      lng_ref, lnb_ref, ws_ref, bias_ref,
                      cos_ref, sin_ref, again_ref, ared_ref, aexp_ref,
                      cw_ref, dred_ref, dexp_ref, al_ref, dt_ref,
                      ya_out, q_out, k_out, v_out, dq_out, dk_out, dv_out, g_out, z_out,
                      xe_ref, *, tm, ncb, nblk):
    i = pl.program_id(1)
    seg_start = jnp.logical_or(i == 0, i == ncb)
    seg_end = jnp.logical_or(i == ncb - 1, i == nblk - 1)
    rows = jnp.concatenate([xp_ref[0], x_ref[0], xn_ref[0]], axis=0)
    hf = _norm(rows) * (1.0 + sc_ref[...]) + sh_ref[...]
    h = hf[HALO:HALO + tm].astype(BF16)
    o_b, o_q, o_z, o_g = A_COLS, A_COLS + B_COLS, A_COLS + B_COLS + QKV_COLS, IN_COLS - GATE_COLS
    xe_ref[...] = _dot(hf.astype(BF16), w_ref[:, o_q:o_z])
    xe_ref[0:HALO, :] = jnp.where(seg_start, 0.0, xe_ref[0:HALO, :])
    xe_ref[HALO + tm:2 * HALO + tm, :] = jnp.where(seg_end, 0.0, xe_ref[HALO + tm:2 * HALO + tm, :])
    _gmlp_rows(_dot(h, w_ref[:, :o_b]), lng_ref, lnb_ref, ws_ref, bias_ref, ya_out)
    _attn_prep_rows(_dot(h, w_ref[:, o_b:o_q]), cos_ref, sin_ref, again_ref, ared_ref, aexp_ref,
                    q_out, k_out, v_out)
    z_out[0] = _dot(h, w_ref[:, o_z:o_g]).astype(BF16)
    _dn_prep_rows(xe_ref, _dot(h, w_ref[:, o_g:]), cw_ref, dred_ref, dexp_ref, al_ref, dt_ref,
                  dq_out, dk_out, dv_out, g_out, tm)


def _pre_mixer(xcat, modl, w_in_p, ln_g, ln_b, w_s, b_s, cos_t, sin_t, q_g, k_g, conv_w, a_log, dt_bias,
               tm, ncb, ctx_row):
    bsz, t, d = xcat.shape
    nblk = t // tm
    hb = tm // HALO
    nqk = ATT_WIDTH + KV_WIDTH
    bias = jnp.repeat(b_s.T, HEAD_DIM, axis=1)
    gain = jnp.concatenate([jnp.tile(q_g, ATT_Q_HEADS), jnp.tile(k_g, ATT_KV_HEADS)]).reshape(1, nqk)
    ared, aexp = _head_maps(nqk, 1.0 / HEAD_DIM)
    dred, dexp = _head_maps(2 * DN_WIDTH, 1.0)
    pad = lambda v: jnp.zeros((1, LANES), F32).at[0, :2 * DN_HEADS].set(v.reshape(-1))
    mspec = lambda c: pl.BlockSpec((None, None, 1, d), lambda b, i: (jnp.where(i < ncb, ctx_row, b), c, 0, 0))
    rspec = lambda w: pl.BlockSpec((1, tm, w), lambda b, i: (b, i, 0))
    hspec = lambda nh, w: pl.BlockSpec((1, nh, tm, w), lambda b, i: (b, 0, i, 0))
    heads = ((ATT_Q_HEADS, HEAD_DIM), (ATT_KV_HEADS, HEAD_DIM), (ATT_KV_HEADS, LANES))
    rows_out = ((DN_WIDTH, F32), (DN_WIDTH, F32), (DN_WIDTH, F32), (LANES, F32), (DN_WIDTH, BF16))
    return pl.pallas_call(
        functools.partial(_pre_mixer_kernel, tm=tm, ncb=ncb, nblk=nblk),
        grid=(bsz, nblk),
        in_specs=[
            rspec(d),
            pl.BlockSpec((1, HALO, d), lambda b, i: (b, jnp.maximum(i * hb - 1, 0), 0)),
            pl.BlockSpec((1, HALO, d), lambda b, i: (b, jnp.minimum((i + 1) * hb, nblk * hb - 1), 0)),
            mspec(0), mspec(1),
            pl.BlockSpec(w_in_p.shape, lambda b, i: (0, 0), pipeline_mode=pl.Buffered(1)),
            _const_spec((1, A_WIDTH)), _const_spec((1, A_WIDTH)), _const_spec(w_s.shape), _const_spec(bias.shape),
            pl.BlockSpec((tm, nqk), lambda b, i: (i, 0)), pl.BlockSpec((tm, nqk), lambda b, i: (i, 0)),
            _const_spec((1, nqk)), _const_spec(ared.shape), _const_spec(aexp.shape),
            _const_spec(conv_w.shape), _const_spec(dred.shape), _const_spec(dexp.shape),
            _const_spec((1, LANES)), _const_spec((1, LANES)),
        ],
        out_specs=[rspec(A_WIDTH)] + [hspec(nh, w) for nh, w in heads] + [rspec(w) for w, _ in rows_out],
        out_shape=[jax.ShapeDtypeStruct((bsz, t, A_WIDTH), BF16)]
        + [jax.ShapeDtypeStruct((bsz, nh, t, w), BF16) for nh, w in heads]
        + [jax.ShapeDtypeStruct((bsz, t, w), dt) for w, dt in rows_out],
        scratch_shapes=[pltpu.VMEM((tm + 2 * HALO, QKV_COLS), F32)],
        compiler_params=_params(2),
        name="pre_mixer",
    )(xcat, xcat, xcat, modl, modl, w_in_p,
      ln_g.reshape(1, -1), ln_b.reshape(1, -1), w_s.astype(BF16), bias,
      cos_t, sin_t, gain, ared, aexp, conv_w, dred, dexp, pad(a_log), pad(dt_bias))


TILE_A = 4 * HEAD_DIM


def _split3(x):
    hi = x.astype(BF16)
    r = x - hi.astype(F32)
    mid = r.astype(BF16)
    lo = (r - mid.astype(F32)).astype(BF16)
    return hi, mid, lo


def _head_id(shape, axis):
    return jnp.right_shift(lax.broadcasted_iota(jnp.int32, shape, axis), 6)


class _Packed:
    def __init__(self):
        c, w = DN_CHUNK, DN_WIDTH
        wb = w - TILE_A
        self.mask_a = _head_id((TILE_A, TILE_A), 0) == _head_id((TILE_A, TILE_A), 1)
        self.mask_b = _head_id((wb, wb), 0) == _head_id((wb, wb), 1)
        self.ri = lax.broadcasted_iota(jnp.int32, (c, w), 0)
        self.cj = lax.broadcasted_iota(jnp.int32, (c, w), 1) & (HEAD_DIM - 1)
        self.eye = (self.ri == self.cj).astype(F32)
        self.incl = (self.cj <= self.ri, self.cj >= self.ri)
        self.strict = (self.cj < self.ri, self.cj > self.ri)
        r64 = lax.broadcasted_iota(jnp.int32, (c, c), 0)
        c64 = lax.broadcasted_iota(jnp.int32, (c, c), 1)
        self.order = ((c64 <= r64).astype(BF16), (c64 >= r64).astype(BF16))
        self.blk = {b: jnp.right_shift(self.ri, int(math.log2(b))) == jnp.right_shift(self.cj, int(math.log2(b)))
                    for b in (8, 16, 32)}

    def weights(self, y):
        y16 = y.astype(BF16)
        ya = jnp.concatenate([y16[:, :TILE_A]] * 4, axis=0)
        yb = jnp.concatenate([y16[:, TILE_A:]] * 2, axis=0)
        zero = jnp.zeros((), BF16)
        return jnp.where(self.mask_a, ya, zero), jnp.where(self.mask_b, yb, zero)

    def dot(self, x, y):
        wa, wb = self.weights(y)
        x16 = x.astype(BF16)
        return jnp.concatenate([_dot(x16[:, :TILE_A], wa), _dot(x16[:, TILE_A:], wb)], axis=-1)

    def dot_nt(self, x, y):
        wa, wb = self.weights(y)
        x16 = x.astype(BF16)
        return jnp.concatenate([_dot_nt(x16[:, :TILE_A], wa), _dot_nt(x16[:, TILE_A:], wb)], axis=-1)

    def dot_tn(self, x, y):
        x16, y16 = x.astype(BF16), y.astype(BF16)
        ta = _dot_tn(x16[:, :TILE_A], y16[:, :TILE_A])
        tb = _dot_tn(x16[:, TILE_A:], y16[:, TILE_A:])
        return jnp.where(self.mask_a, ta, 0.0), jnp.where(self.mask_b, tb, 0.0)

    def tri_inverse(self, lows):
        c = DN_CHUNK
        nb = [-jnp.where(self.blk[8], low, 0.0) for low in lows]
        x = [self.eye + a for a in nb]
        pw = [self.dot(a, a) for a in nb]
        st = [self.dot(jnp.concatenate([xa, p], axis=0), p) for xa, p in zip(x, pw)]
        x = [xa + s[:c] for xa, s in zip(x, st)]
        x = [xa + self.dot(xa, s[c:]) for xa, s in zip(x, st)]
        for b in (8, 16, 32):
            inner = self.blk[b]
            outer = self.blk[2 * b] if 2 * b in self.blk else None
            off = jnp.logical_not(inner) if outer is None else jnp.logical_and(outer, jnp.logical_not(inner))
            xc = [self.dot(xa, jnp.where(off, low, 0.0)) for xa, low in zip(x, lows)]
            x = [xa - self.dot(t, xa) for xa, t in zip(x, xc)]
        return x


def _spread_heads(x, first):
    c = x.shape[0]
    low_half = lax.broadcasted_iota(jnp.int32, (c, LANES), 1) < HEAD_DIM
    tiles = []
    for j in range(0, DN_HEADS, 2):
        a = jnp.broadcast_to(x[:, first + j:first + j + 1], (c, LANES))
        b = jnp.broadcast_to(x[:, first + j + 1:first + j + 2], (c, LANES))
        tiles.append(jnp.where(low_half, a, b))
    return jnp.concatenate(tiles, axis=-1)


def _dn_chunk_terms(pk, dirs, q, k, v, gates):
    c = DN_CHUNK
    incl = [pk.incl[d] for d in dirs]
    strict = [pk.strict[d] for d in dirs]
    gc = [sum(_dot(pk.order[d], p) for p in _split3(g)) for d, g in zip(dirs, gates)]
    gcx = [_spread_heads(a, d * DN_HEADS) for d, a in zip(dirs, gc)]
    bx = [_spread_heads(g, (2 + d) * DN_HEADS) for d, g in zip(dirs, gates)]
    gtot = [a[c - 1:c, :] if d == 0 else a[0:1, :] for d, a in zip(dirs, gcx)]
    grow = [jnp.sum(pk.eye * a, 0, keepdims=True) for a in gcx]
    decay = [jnp.where(m, jnp.exp(jnp.minimum(a - r, 0.0)), 0.0) for m, a, r in zip(incl, gcx, grow)]
    kbeta = [a * b for a, b in zip(k, bx)]
    kq = [pk.dot_nt(jnp.concatenate([a, b], axis=0), kk) for a, b, kk in zip(kbeta, q, k)]
    low = [jnp.where(m, a[:c] * dc, 0.0) for m, a, dc in zip(strict, kq, decay)]
    intra = [jnp.where(m, a[c:] * dc, 0.0) for m, a, dc in zip(incl, kq, decay)]
    tinv = pk.tri_inverse(low)
    eg = [jnp.exp(a) for a in gcx]
    u = [pk.dot(t, a * b) for t, a, b in zip(tinv, v, bx)]
    wm = [pk.dot(t, a * e) for t, a, e in zip(tinv, kbeta, eg)]
    lhs = [jnp.concatenate([a * e, w_], axis=0).astype(BF16) for a, e, w_ in zip(q, eg, wm)]
    k_dec = [a * jnp.exp(gt - gx) for a, gt, gx in zip(k, gtot, gcx)]
    egl = [jnp.exp(gt) for gt in gtot]
    return [dict(lhs=a, u=b, intra=i, k_dec=kd, egl=e) for a, b, i, kd, e in zip(lhs, u, intra, k_dec, egl)]


def _dn_scan_kernel(qf, kf, vf, gf, qb, kb_, vb, gb, of, ob, sa_ref, sb_ref, *, nsub):
    c = DN_CHUNK

    @pl.when(pl.program_id(1) == 0)
    def _():
        sa_ref[...] = jnp.zeros_like(sa_ref)
        sb_ref[...] = jnp.zeros_like(sb_ref)

    pk = _Packed()
    refs = ((qf, kf, vf, gf, of), (qb, kb_, vb, gb, ob))
    items = [(d, j if d == 0 else nsub - 1 - j) for j in range(nsub) for d in (0, 1)]
    rows = [slice(j * c, (j + 1) * c) for _, j in items]
    dirs = [d for d, _ in items]
    terms = _dn_chunk_terms(
        pk, dirs,
        [refs[d][0][0, r, :] for d, r in zip(dirs, rows)],
        [refs[d][1][0, r, :] for d, r in zip(dirs, rows)],
        [refs[d][2][0, r, :] for d, r in zip(dirs, rows)],
        [refs[d][3][0, r, :] for d, r in zip(dirs, rows)])
    state = [(sa_ref[d], sb_ref[d]) for d in (0, 1)]
    for d, r, t in zip(dirs, rows, terms):
        sa, sb = state[d]
        lhs = t["lhs"]
        res = jnp.concatenate([_dot(lhs[:, :TILE_A], sa.astype(BF16)), _dot(lhs[:, TILE_A:], sb.astype(BF16))],
                              axis=-1)
        v_new = t["u"] - res[c:]
        refs[d][4][0, r, :] = res[:c] + pk.dot(t["intra"], v_new)
        ta, tb = pk.dot_tn(t["k_dec"], v_new)
        state[d] = (sa * t["egl"][:, :TILE_A] + ta, sb * t["egl"][:, TILE_A:] + tb)
    for d in (0, 1):
        sa_ref[d], sb_ref[d] = state[d]


DN_BLOCKS = (4 * DN_CHUNK, 2 * DN_CHUNK)


def _dn_scan(q, k, v, g, tc):
    bsz, t, w = q.shape
    c = next(blk for blk in DN_BLOCKS if tc % blk == 0 and t % blk == 0)
    nct, ncx = t // c, tc // c

    def bwd_blk(s):
        return jnp.where(s < ncx, ncx - 1 - s, nct - 1 - s + ncx)

    fspec = lambda width: pl.BlockSpec((1, c, width), lambda b, s: (b, s, 0))
    bspec = lambda width: pl.BlockSpec((1, c, width), lambda b, s: (b, bwd_blk(s), 0))
    return pl.pallas_call(
        functools.partial(_dn_scan_kernel, nsub=c // DN_CHUNK),
        grid=(bsz, nct),
        in_specs=[fspec(w), fspec(w), fspec(w), fspec(LANES), bspec(w), bspec(w), bspec(w), bspec(LANES)],
        out_specs=[fspec(w), bspec(w)],
        out_shape=[jax.ShapeDtypeStruct((bsz, t, w), F32)] * 2,
        scratch_shapes=[pltpu.VMEM((2, TILE_A, TILE_A), F32), pltpu.VMEM((2, w - TILE_A, w - TILE_A), F32)],
        compiler_params=_params(2),
        name="dn_scan",
    )(q, k, v, g, q, k, v, g)


def _post_kernel(x_ref, ya_ref, yb0_ref, yb1_ref, of_ref, ob_ref, z_ref, ngain_ref, nred_ref, nexp_ref,
                 g1_ref, sh2_ref, sc2_ref, g2_ref,
                 wa_ref, wb0_ref, wb1_ref, wc_ref, l1g_ref, l1b_ref, l2g_ref, l2b_ref,
                 wup_ref, wdn_ref, o_ref, *, alpha, f_chunk):
    o = of_ref[0] + ob_ref[0]
    yc = (o * _head_rsqrt(o, nred_ref, nexp_ref) * ngain_ref[...] * _silu(z_ref[0].astype(F32))).astype(BF16)
    branch = (_dot(ya_ref[0], wa_ref[...]) + _dot(yb0_ref[0, 0], wb0_ref[...])
              + _dot(yb1_ref[0, 0], wb1_ref[...]) + _dot(yc, wc_ref[...]))
    x1 = _norm(alpha * x_ref[0] + g1_ref[...] * branch) * l1g_ref[...] + l1b_ref[...]
    h = (_norm(x1) * (1.0 + sc2_ref[...]) + sh2_ref[...]).astype(BF16)
    d_ff = wup_ref.shape[1]
    m = None
    for f in range(0, d_ff, f_chunk):
        up = jnp.maximum(_dot(h, wup_ref[:, f:f + f_chunk]), 0.0)
        part = _dot((up * up).astype(BF16), wdn_ref[f:f + f_chunk, :])
        m = part if m is None else m + part
    o_ref[0] = _norm(alpha * x1 + g2_ref[...] * m) * l2g_ref[...] + l2b_ref[...]


def _post(xcat, ya, yb, o_f, o_b, pz, norm_g, modl, w_out, ln1_g, ln1_b, ln2_g, ln2_b, w_up, w_down,
          tm, ncb, ctx_row, alpha, need_ctx):
    bsz, t, d = xcat.shape
    ngain = jnp.tile(norm_g, DN_HEADS).reshape(1, DN_WIDTH)
    nred, nexp = _head_maps(DN_WIDTH, 1.0 / HEAD_DIM)
    first = 0 if need_ctx else ncb
    yb_first = first - (t - yb.shape[2]) // tm
    gw = ATT_GROUP * HEAD_DIM
    o1 = A_WIDTH
    wa, wb0, wb1, wc = w_out[:o1], w_out[o1:o1 + gw], w_out[o1 + gw:o1 + 2 * gw], w_out[o1 + 2 * gw:]
    mspec = lambda c: pl.BlockSpec(
        (None, None, 1, d), lambda b, i: (jnp.where(i + first < ncb, ctx_row, b), c, 0, 0))
    rspec = lambda w: pl.BlockSpec((1, tm, w), lambda b, i: (b, i + first, 0))
    vec = lambda a: a.reshape(1, d)
    wspec = lambda a: pl.BlockSpec(a.shape, lambda b, i: (0, 0), pipeline_mode=pl.Buffered(1))
    weights = [w.astype(BF16) for w in (wa, wb0, wb1, wc)]
    w_up, w_down = w_up.astype(BF16), w_down.astype(BF16)
    return pl.pallas_call(
        functools.partial(_post_kernel, alpha=alpha, f_chunk=1024),
        grid=(bsz, t // tm - first),
        in_specs=[
            rspec(d), rspec(A_WIDTH),
            pl.BlockSpec((1, 1, tm, gw), lambda b, i: (b, 0, i + yb_first, 0)),
            pl.BlockSpec((1, 1, tm, gw), lambda b, i: (b, 1, i + yb_first, 0)),
            rspec(DN_WIDTH), rspec(DN_WIDTH), rspec(DN_WIDTH),
            _const_spec(ngain.shape), _const_spec(nred.shape), _const_spec(nexp.shape),
            mspec(2), mspec(3), mspec(4), mspec(5),
            *[wspec(w) for w in weights],
            _const_spec((1, d)), _const_spec((1, d)), _const_spec((1, d)), _const_spec((1, d)),
            wspec(w_up), wspec(w_down),
        ],
        out_specs=pl.BlockSpec((1, tm, d), lambda b, i: (b, i, 0)),
        out_shape=jax.ShapeDtypeStruct((bsz, t - first * tm, d), F32),
        compiler_params=_params(2),
        name="post_mixer",
    )(xcat, ya, yb, yb, o_f, o_b, pz, ngain, nred, nexp, modl, modl, modl, modl, *weights,
      vec(ln1_g), vec(ln1_b), vec(ln2_g), vec(ln2_b), w_up, w_down)


def _rope_tables(tc, tl):
    pos = jnp.arange(tl)
    row = (pos // GRID_W).astype(F32)
    col = (pos % GRID_W).astype(F32)
    half = HEAD_DIM // 2
    inv = 1.0 / (ROPE_THETA ** (jnp.arange(0, half, 2, dtype=F32) / half))
    ar, ac = row[:, None] * inv, col[:, None] * inv
    cos = jnp.concatenate([jnp.cos(ar), jnp.cos(ar), jnp.cos(ac), jnp.cos(ac)], -1)
    sin = jnp.concatenate([-jnp.sin(ar), jnp.sin(ar), -jnp.sin(ac), jnp.sin(ac)], -1)
    cos = jnp.concatenate([jnp.ones((tc, HEAD_DIM), F32), cos], 0)
    sin = jnp.concatenate([jnp.zeros((tc, HEAD_DIM), F32), sin], 0)
    scale = HEAD_DIM ** -0.5 * math.log2(math.e)
    tile =lambda a: jnp.concatenate([jnp.tile(a, (1, ATT_Q_HEADS)) * scale, jnp.tile(a, (1, ATT_KV_HEADS))], -1)
    return tile(cos), tile(sin)


def kernel(x, c, ctx, c_ctx, mod_w, mod_b, w_in, w_out, gmlp_ln_g, gmlp_ln_b, gmlp_w_s, gmlp_b_s,
           attn_q_g, attn_k_g, dn_conv_w, dn_a_log, dn_dt_bias, dn_norm_g,
           ln1_g, ln1_b, ln2_g, ln2_b, w_up, w_down):
    bsz, tl, d = x.shape
    tc = ctx.shape[1]
    depth = mod_w.shape[0]
    assert bsz < MOD_ROWS and tl % GRID_W == 0 and tc % MLP_CHUNK == 0 and tl % MLP_CHUNK == 0
    tm = 256 if tc % 256 == 0 else MLP_CHUNK
    ncb = tc // tm
    ctx_row = bsz
    alpha = (2 * depth) ** 0.25

    cs = jnp.zeros((MOD_ROWS, d), F32).at[:bsz].set(c).at[ctx_row].set(c_ctx)
    mod = _modulation(cs, mod_w, mod_b).reshape(depth, MOD_ROWS, 6, 1, d)
    cos_t, sin_t = _rope_tables(tc, tl)
    xcat = jnp.concatenate([ctx, x], axis=1)

    for l in range(depth):
        need_ctx = l < depth - 1
        w_in_p = jnp.pad(w_in[l], ((0, 0), (0, IN_COLS_PAD - IN_COLS))).astype(BF16)
        ya, qa, ka, va, dq, dk, dv, dg, pz = _pre_mixer(
            xcat, mod[l], w_in_p, gmlp_ln_g[l], gmlp_ln_b[l], gmlp_w_s[l], gmlp_b_s[l], cos_t, sin_t,
            attn_q_g[l], attn_k_g[l], dn_conv_w[l], dn_a_log[l], dn_dt_bias[l], tm, ncb, ctx_row)
        yb_ctx, yb = _attention(qa, ka, va, tc, need_ctx)
        if need_ctx:
            yb = jnp.concatenate([yb_ctx, yb], axis=2)
        o_f, o_b = _dn_scan(dq, dk, dv, dg, tc)
        xcat = _post(xcat, ya, yb, o_f, o_b, pz, dn_norm_g[l], mod[l], w_out[l], ln1_g[l], ln1_b[l], ln2_g[l], ln2_b[l],
                     w_up[l], w_down[l], tm, ncb, ctx_row, alpha, need_ctx)
    return xcat
```

```python
import functools
import math

import jax
import jax.numpy as jnp
from jax import lax
from jax.experimental import pallas as pl
from jax.experimental.pallas import tpu as pltpu

F32 = jnp.float32
BF16 = jnp.bfloat16

GRID_W = 64
HEAD_DIM = 64
A_GROUPS = 4
A_WIDTH = A_GROUPS * HEAD_DIM
MLP_CHUNK = 128
ATT_Q_HEADS = 6
ATT_KV_HEADS = 2
ATT_GROUP = ATT_Q_HEADS // ATT_KV_HEADS
ATT_WIDTH = ATT_Q_HEADS * HEAD_DIM
KV_WIDTH = ATT_KV_HEADS * HEAD_DIM
ROPE_THETA = 10000.0
DN_HEADS = 6
DN_WIDTH = DN_HEADS * HEAD_DIM
DN_CONV = 5
DN_CHUNK = 64
A_COLS = 2 * A_WIDTH
B_COLS = ATT_WIDTH + 2 * KV_WIDTH
QKV_COLS = 3 * DN_WIDTH
GATE_COLS = 4 * DN_HEADS
IN_COLS = A_COLS + B_COLS + QKV_COLS + DN_WIDTH + GATE_COLS
LANES = 128
SUBLANES = 8
IN_COLS_PAD = IN_COLS - GATE_COLS + LANES
EPS = 1e-6
MOD_ROWS = 8
VMEM_LIMIT = 56 * 1024 * 1024


def _dot(a, b):
    return jnp.dot(a, b, preferred_element_type=F32)


def _dot_nt(a, b):
    return lax.dot_general(a, b, (((1,), (1,)), ((), ())), preferred_element_type=F32)


def _dot_tn(a, b):
    return lax.dot_general(a, b, (((0,), (0,)), ((), ())), preferred_element_type=F32)


def _split(x):
    hi = x.astype(BF16)
    lo = (x - hi.astype(F32)).astype(BF16)
    return hi, lo


def _dot_x3(a, b):
    ah, al = _split(a)
    bh, bl = _split(b)
    return _dot(ah, bh) + (_dot(ah, bl) + _dot(al, bh))


def _dot_lhs_split(x, m):
    hi, lo = _split(x)
    return _dot(hi, m) + _dot(lo, m)


def _norm(x):
    mu = jnp.mean(x, -1, keepdims=True)
    xc = x - mu
    var = jnp.mean(xc * xc, -1, keepdims=True)
    return xc * lax.rsqrt(var + EPS)


def _sigmoid(x):
    return 1.0 / (1.0 + jnp.exp(-x))


def _silu(x):
    return x * _sigmoid(x)


def _params(n_grid, vmem=VMEM_LIMIT):
    return pltpu.CompilerParams(dimension_semantics=("arbitrary",) * n_grid, vmem_limit_bytes=vmem)


def _const_spec(shape):
    nd = len(shape)
    return pl.BlockSpec(shape, lambda *_: (0,) * nd)


def _mod_kernel(cs_ref, w_ref, b_ref, o_ref):
    cs = cs_ref[...]
    o_ref[0] = _dot_x3(_silu(cs), w_ref[0]) + b_ref[0]


def _modulation(cs, mod_w, mod_b):
    depth, d, n = mod_w.shape
    tn = 1536
    return pl.pallas_call(
        _mod_kernel,
        grid=(depth, n // tn),
        in_specs=[
            pl.BlockSpec((MOD_ROWS, d), lambda l, j: (0, 0)),
            pl.BlockSpec((1, d, tn), lambda l, j: (l, 0, j)),
            pl.BlockSpec((1, 1, tn), lambda l, j: (l, 0, j)),
        ],
        out_specs=pl.BlockSpec((1, MOD_ROWS, tn), lambda l, j: (l, 0, j)),
        out_shape=jax.ShapeDtypeStruct((depth, MOD_ROWS, n), F32),
        compiler_params=_params(2),
        name="modulation",
    )(cs, mod_w, mod_b.reshape(depth, 1, n))


def _gmlp_rows(pa, lng_ref, lnb_ref, ws_ref, bias_ref, o_ref):
    for ci in range(pa.shape[0] // MLP_CHUNK):
        rows = slice(ci * MLP_CHUNK, (ci + 1) * MLP_CHUNK)
        p = pa[rows]
        a = 0.5 * p * (1.0 + lax.erf(p * (2.0 ** -0.5)))
        u = a[:, :A_WIDTH]
        v = _norm(a[:, A_WIDTH:]) * lng_ref[...] + lnb_ref[...]
        vb = v.astype(BF16)
        mixed = jnp.concatenate(
            [_dot(ws_ref[g], vb[:, g * HEAD_DIM:(g + 1) * HEAD_DIM]) for g in range(A_GROUPS)], axis=-1)
        o_ref[0, rows, :] = (u * (mixed + bias_ref[...])).astype(BF16)


def _swap_rope_pairs(x):
    n = x.shape[-1]
    lane = lax.broadcasted_iota(jnp.int32, x.shape, x.ndim - 1)
    first = (lane & 31) < 16
    return jnp.where(first, pltpu.roll(x, n - 16, x.ndim - 1), pltpu.roll(x, 16, x.ndim - 1))


def _head_rsqrt(x, red_ref, exp_ref):
    s = _dot_lhs_split(x * x, red_ref[...])
    return _dot_lhs_split(lax.rsqrt(s + EPS), exp_ref[...])


def _head_maps(width, value):
    head = jnp.arange(width) // HEAD_DIM
    col = jnp.arange(LANES)
    red = jnp.where(head[:, None] == col[None, :], value, 0.0).astype(BF16)
    return red, (col[:, None] == head[None, :]).astype(BF16)


def _attn_prep_rows(pb, cos_ref, sin_ref, gain_ref, red_ref, exp_ref, q_out, k_out, v_out):
    nqk = ATT_WIDTH + KV_WIDTH
    qk = pb[:, :nqk]
    qk = qk * _head_rsqrt(qk, red_ref, exp_ref) * gain_ref[...]
    r = qk * cos_ref[...] + _swap_rope_pairs(qk) * sin_ref[...]
    for h in range(ATT_Q_HEADS):
        q_out[0, h] = r[:, h * HEAD_DIM:(h + 1) * HEAD_DIM].astype(BF16)
    vv = pb[:, nqk:nqk + KV_WIDTH]
    lane = lax.broadcasted_iota(jnp.int32, vv.shape, 1)
    for h in range(ATT_KV_HEADS):
        k_out[0, h] = r[:, ATT_WIDTH + h * HEAD_DIM:ATT_WIDTH + (h + 1) * HEAD_DIM].astype(BF16)
        vh = vv if h == 0 else pltpu.roll(vv, KV_WIDTH - h * HEAD_DIM, 1)
        v_out[0, h] = jnp.where(lane < HEAD_DIM, vh, jnp.where(lane == HEAD_DIM, 1.0, 0.0)).astype(BF16)


KV_BLOCK = 256


def _key_blocks(nkeys):
    return [(s0, min(KV_BLOCK, nkeys - s0)) for s0 in range(0, nkeys, KV_BLOCK)]


def _store_heads(o_ref, acc, tq):
    o = acc[:, :HEAD_DIM] / acc[:, HEAD_DIM:HEAD_DIM + 1]
    for g in range(ATT_GROUP):
        o_ref[0, 0, :, g * HEAD_DIM:(g + 1) * HEAD_DIM] = o[g * tq:(g + 1) * tq].astype(BF16)


def _attn_lat_kernel(q_ref, k_ref, v_ref, oe_ref, oo_ref, sa_ref, sb_ref, ma_ref, mb_ref, *, tq):
    t = k_ref.shape[2]

    @pl.when(pl.program_id(2) == 0)
    def _():
        sb_ref[...] = jnp.zeros(sb_ref.shape, F32)
        mb_ref[...] = jnp.zeros(mb_ref.shape, F32)

    def half(q, s_new, m_new, s_old, m_old, o_ref):
        m = jnp.max(m_old[...], -1, keepdims=True)
        mx = None
        acc = None
        for s0, n in _key_blocks(t):
            s = _dot_nt(q, k_ref[0, 0, s0:s0 + n, :])
            s_new[:, s0:s0 + n] = s
            for l0 in range(0, n, LANES):
                part = s[:, l0:l0 + LANES]
                mx = part if mx is None else jnp.maximum(mx, part)
            p = jnp.exp2(s_old[:, s0:s0 + n] - m).astype(BF16)
            part = _dot(p, v_ref[0, 0, s0:s0 + n, :])
            acc = part if acc is None else acc + part
        m_new[...] = mx
        _store_heads(o_ref, acc, tq)

    rows = ATT_GROUP * tq
    half(q_ref[0, :, :tq, :].reshape(rows, HEAD_DIM), sa_ref, ma_ref, sb_ref, mb_ref, oo_ref)
    half(q_ref[0, :, tq:, :].reshape(rows, HEAD_DIM), sb_ref, mb_ref, sa_ref, ma_ref, oe_ref)


def _attn_ctx_kernel(q_ref, k_ref, v_ref, o_ref, *, tc):
    q = q_ref[0].reshape(ATT_GROUP * tc, HEAD_DIM)
    s = _dot_nt(q, k_ref[0, 0])
    p = jnp.exp2(s - jnp.max(s, -1, keepdims=True)).astype(BF16)
    _store_heads(o_ref, _dot(p, v_ref[0, 0]), tc)


def _attention(q, k, v, tc, need_ctx, tq=128):
    bsz, _, t, _ = q.shape
    gw = ATT_GROUP * HEAD_DIM
    assert tc % (2 * tq) == 0 and (t - tc) % (2 * tq) == 0
    first = tc // (2 * tq)
    n2 = (t - tc) // (2 * tq)
    rows = ATT_GROUP * tq
    half_shape = jax.ShapeDtypeStruct((bsz, ATT_KV_HEADS, n2 * tq, gw), BF16)
    y_even, y_odd = pl.pallas_call(
        functools.partial(_attn_lat_kernel, tq=tq),
        grid=(bsz, ATT_KV_HEADS, n2 + 1),
        in_specs=[
            pl.BlockSpec((1, ATT_GROUP, 2 * tq, HEAD_DIM),
                         lambda b, j, i: (b, j, first + jnp.minimum(i, n2 - 1), 0)),
            pl.BlockSpec((1, 1, t, HEAD_DIM), lambda b, j, i: (b, j, 0, 0)),
            pl.BlockSpec((1, 1, t, LANES), lambda b, j, i: (b, j, 0, 0)),
        ],
        out_specs=[pl.BlockSpec((1, 1, tq, gw), lambda b, j, i: (b, j, jnp.minimum(i, n2 - 1), 0)),
                   pl.BlockSpec((1, 1, tq, gw), lambda b, j, i: (b, j, jnp.maximum(i - 1, 0), 0))],
        out_shape=[half_shape, half_shape],
        scratch_shapes=[pltpu.VMEM((rows, t), F32), pltpu.VMEM((rows, t), F32),
                        pltpu.VMEM((rows, LANES), F32), pltpu.VMEM((rows, LANES), F32)],
        compiler_params=_params(3),
        name="attention",
    )(q, k, v)
    y_lat = jnp.stack([y_even.reshape(bsz, ATT_KV_HEADS, n2, tq, gw),
                       y_odd.reshape(bsz, ATT_KV_HEADS, n2, tq, gw)], axis=3).reshape(bsz, ATT_KV_HEADS, t - tc, gw)
    if not need_ctx:
        return None, y_lat
    y_ctx = pl.pallas_call(
        functools.partial(_attn_ctx_kernel, tc=tc),
        grid=(bsz, ATT_KV_HEADS),
        in_specs=[
            pl.BlockSpec((1, ATT_GROUP, tc, HEAD_DIM), lambda b, j: (b, j, 0, 0)),
            pl.BlockSpec((1, 1, tc, HEAD_DIM), lambda b, j: (b, j, 0, 0)),
            pl.BlockSpec((1, 1, tc, LANES), lambda b, j: (b, j, 0, 0)),
        ],
        out_specs=pl.BlockSpec((1, 1, tc, gw), lambda b, j: (b, j, 0, 0)),
        out_shape=jax.ShapeDtypeStruct((bsz, ATT_KV_HEADS, tc, gw), BF16),
        compiler_params=_params(2),
        name="attention_ctx",
    )(q, k, v)
    return y_ctx, y_lat


HALO = SUBLANES


def _dn_prep_rows(xe_ref, pg, cw_ref, red_ref, exp_ref, al_ref, dt_ref, qkv_out, g_out, tm):
    first = HALO - DN_CONV // 2
    acc = cw_ref[0:1, :] * xe_ref[first:first + tm, :]
    for j in range(1, DN_CONV):
        acc = acc + cw_ref[j:j + 1, :] * xe_ref[first + j:first + j + tm, :]
    y = _silu(acc)
    qk = y[:, :2 * DN_WIDTH]
    qk = qk * _head_rsqrt(qk, red_ref, exp_ref)
    qkv_out[0, :, :DN_WIDTH] = qk[:, :DN_WIDTH] * (HEAD_DIM ** -0.5)
    qkv_out[0, :, DN_WIDTH:2 * DN_WIDTH] = qk[:, DN_WIDTH:]
    qkv_out[0, :, 2 * DN_WIDTH:] = y[:, 2 * DN_WIDTH:]
    z = pg + dt_ref[...]
    softplus = jnp.maximum(z, 0.0) + jnp.log1p(jnp.exp(-jnp.abs(z)))
    lane = lax.broadcasted_iota(jnp.int32, pg.shape, 1)
    g_out[0] = jnp.where(lane < 2 * DN_HEADS, -jnp.exp(al_ref[...]) * softplus, _sigmoid(pg))


def _pre_mixer_kernel(*refs, tm, ncb, nblk, nsrc):
    rows_refs, rest = refs[:3 * nsrc], refs[3 * nsrc:]
    (sh_ref, sc_ref, w_ref, lng_ref, lnb_ref, ws_ref, bias_ref,
     cos_ref, sin_ref, again_ref, ared_ref, aexp_ref, cw_ref, dred_ref, dexp_ref, al_ref, dt_ref,
     ya_out, q_out, k_out, v_out, dqkv_out, g_out, z_out, xe_ref) = rest
    i = pl.program_id(1)
    seg_start = jnp.logical_or(i == 0, i == ncb)
    seg_end = jnp.logical_or(i == ncb - 1, i == nblk - 1)
    cand = [jnp.concatenate([rows_refs[3 * s + 1][0], rows_refs[3 * s][0], rows_refs[3 * s + 2][0]], axis=0)
            for s in range(nsrc)]
    rows = cand[0] if nsrc == 1 else jnp.where(i < ncb, cand[0], cand[1])
    hf = _norm(rows) * (1.0 + sc_ref[...]) + sh_ref[...]
    h = hf[HALO:HALO + tm].astype(BF16)
    o_b, o_q, o_z, o_g = A_COLS, A_COLS + B_COLS, A_COLS + B_COLS + QKV_COLS, IN_COLS - GATE_COLS
    xe_ref[...] = _dot(hf.astype(BF16), w_ref[:, o_q:o_z])
    xe_ref[0:HALO, :] = jnp.where(seg_start, 0.0, xe_ref[0:HALO, :])
    xe_ref[HALO + tm:2 * HALO + tm, :] = jnp.where(seg_end, 0.0, xe_ref[HALO + tm:2 * HALO + tm, :])
    _gmlp_rows(_dot(h, w_ref[:, :o_b]), lng_ref, lnb_ref, ws_ref, bias_ref, ya_out)
    _attn_prep_rows(_dot(h, w_ref[:, o_b:o_q]), cos_ref, sin_ref, again_ref, ared_ref, aexp_ref,
                    q_out, k_out, v_out)
    z_out[0] = _dot(h, w_ref[:, o_z:o_g]).astype(BF16)
    _dn_prep_rows(xe_ref, _dot(h, w_ref[:, o_g:]), cw_ref, dred_ref, dexp_ref, al_ref, dt_ref,
                  dqkv_out, g_out, tm)


def _row_block_sources(src, tm, ncb, first=0, halo=False):
    parts = src if isinstance(src, tuple) else (src,)
    starts = (0, ncb) if len(parts) == 2 else (0,)
    hb = tm // HALO
    arrays, specs = [], []
    for arr, start in zip(parts, starts):
        d = arr.shape[-1]
        nb = arr.shape[1] // tm

        def blk(i, start=start, nb=nb):
            return jnp.clip(i + first - start, 0, nb - 1)

        arrays.append(arr)
        specs.append(pl.BlockSpec((1, tm, d), lambda b, i, blk=blk: (b, blk(i), 0)))
        if halo:
            arrays += [arr, arr]
            specs.append(pl.BlockSpec((1, HALO, d), lambda b, i, blk=blk: (b, jnp.maximum(blk(i) * hb - 1, 0), 0)))
            specs.append(pl.BlockSpec(
                (1, HALO, d), lambda b, i, blk=blk, nb=nb: (b, jnp.minimum((blk(i) + 1) * hb, nb * hb - 1), 0)))
    return arrays, specs


def _pre_mixer(src, modl, w_in_p, ln_g, ln_b, w_s, b_s, cos_t, sin_t, q_g, k_g, conv_w, a_log, dt_bias,
               tm, ncb, ctx_row):
    parts = src if isinstance(src, tuple) else (src,)
    bsz, _, d = parts[0].shape
    t = sum(p.shape[1] for p in parts)
    nblk = t // tm
    row_arrays, row_specs = _row_block_sources(src, tm, ncb, halo=True)
    nqk = ATT_WIDTH + KV_WIDTH
    bias = jnp.repeat(b_s.T, HEAD_DIM, axis=1)
    gain = jnp.concatenate([jnp.tile(q_g, ATT_Q_HEADS), jnp.tile(k_g, ATT_KV_HEADS)]).reshape(1, nqk)
    ared, aexp = _head_maps(nqk, 1.0 / HEAD_DIM)
    dred, dexp = _head_maps(2 * DN_WIDTH, 1.0)
    pad = lambda v: jnp.zeros((1, LANES), F32).at[0, :2 * DN_HEADS].set(v.reshape(-1))
    mspec = lambda c: pl.BlockSpec((None, None, 1, d), lambda b, i: (jnp.where(i < ncb, ctx_row, b), c, 0, 0))
    rspec = lambda w: pl.BlockSpec((1, tm, w), lambda b, i: (b, i, 0))
    hspec = lambda nh, w: pl.BlockSpec((1, nh, tm, w), lambda b, i: (b, 0, i, 0))
    heads = ((ATT_Q_HEADS, HEAD_DIM), (ATT_KV_HEADS, HEAD_DIM), (ATT_KV_HEADS, LANES))
    rows_out = ((QKV_COLS, F32), (LANES, F32), (DN_WIDTH, BF16))
    return pl.pallas_call(
        functools.partial(_pre_mixer_kernel, tm=tm, ncb=ncb, nblk=nblk, nsrc=len(parts)),
        grid=(bsz, nblk),
        in_specs=row_specs + [
            mspec(0), mspec(1),
            pl.BlockSpec(w_in_p.shape, lambda b, i: (0, 0), pipeline_mode=pl.Buffered(1)),
            _const_spec((1, A_WIDTH)), _const_spec((1, A_WIDTH)), _const_spec(w_s.shape), _const_spec(bias.shape),
            pl.BlockSpec((tm, nqk), lambda b, i: (i, 0)), pl.BlockSpec((tm, nqk), lambda b, i: (i, 0)),
            _const_spec((1, nqk)), _const_spec(ared.shape), _const_spec(aexp.shape),
            _const_spec(conv_w.shape), _const_spec(dred.shape), _const_spec(dexp.shape),
            _const_spec((1, LANES)), _const_spec((1, LANES)),
        ],
        out_specs=[rspec(A_WIDTH)] + [hspec(nh, w) for nh, w in heads] + [rspec(w) for w, _ in rows_out],
        out_shape=[jax.ShapeDtypeStruct((bsz, t, A_WIDTH), BF16)]
        + [jax.ShapeDtypeStruct((bsz, nh, t, w), BF16) for nh, w in heads]
        + [jax.ShapeDtypeStruct((bsz, t, w), dt) for w, dt in rows_out],
        scratch_shapes=[pltpu.VMEM((tm + 2 * HALO, QKV_COLS), F32)],
        compiler_params=_params(2),
        name="pre_mixer",
    )(*row_arrays, modl, modl, w_in_p,
      ln_g.reshape(1, -1), ln_b.reshape(1, -1), w_s.astype(BF16), bias,
      cos_t, sin_t, gain, ared, aexp, conv_w, dred, dexp, pad(a_log), pad(dt_bias))


TILE_A = 4 * HEAD_DIM


def _split3(x):
    hi = x.astype(BF16)
    r = x - hi.astype(F32)
    mid = r.astype(BF16)
    lo = (r - mid.astype(F32)).astype(BF16)
    return hi, mid, lo


def _head_id(shape, axis):
    return jnp.right_shift(lax.broadcasted_iota(jnp.int32, shape, axis), 6)


class _Packed:
    def __init__(self):
        c, w = DN_CHUNK, DN_WIDTH
        wb = w - TILE_A
        self.mask_a = _head_id((TILE_A, TILE_A), 0) == _head_id((TILE_A, TILE_A), 1)
        self.mask_b = _head_id((wb, wb), 0) == _head_id((wb, wb), 1)
        self.ri = lax.broadcasted_iota(jnp.int32, (c, w), 0)
        self.cj = lax.broadcasted_iota(jnp.int32, (c, w), 1) & (HEAD_DIM - 1)
        self.eye = (self.ri == self.cj).astype(F32)
        self.incl = (self.cj <= self.ri, self.cj >= self.ri)
        self.strict = (self.cj < self.ri, self.cj > self.ri)
        r64 = lax.broadcasted_iota(jnp.int32, (c, c), 0)
        c64 = lax.broadcasted_iota(jnp.int32, (c, c), 1)
        self.order = ((c64 <= r64).astype(BF16), (c64 >= r64).astype(BF16))
        self.blk = {b: jnp.right_shift(self.ri, int(math.log2(b))) == jnp.right_shift(self.cj, int(math.log2(b)))
                    for b in (8, 16, 32)}

    def weights(self, y):
        y16 = y.astype(BF16)
        ya = jnp.concatenate([y16[:, :TILE_A]] * 4, axis=0)
        yb = jnp.concatenate([y16[:, TILE_A:]] * 2, axis=0)
        zero = jnp.zeros((), BF16)
        return jnp.where(self.mask_a, ya, zero), jnp.where(self.mask_b, yb, zero)

    def dot(self, x, y):
        wa, wb = self.weights(y)
        x16 = x.astype(BF16)
        return jnp.concatenate([_dot(x16[:, :TILE_A], wa), _dot(x16[:, TILE_A:], wb)], axis=-1)

    def dot_nt(self, x, y):
        wa, wb = self.weights(y)
        x16 = x.astype(BF16)
        return jnp.concatenate([_dot_nt(x16[:, :TILE_A], wa), _dot_nt(x16[:, TILE_A:], wb)], axis=-1)

    def dot_tn(self, x, y):
        x16, y16 = x.astype(BF16), y.astype(BF16)
        ta = _dot_tn(x16[:, :TILE_A], y16[:, :TILE_A])
        tb = _dot_tn(x16[:, TILE_A:], y16[:, TILE_A:])
        return jnp.where(self.mask_a, ta, 0.0), jnp.where(self.mask_b, tb, 0.0)

    def tri_inverse(self, lows):
        c = DN_CHUNK
        nb = [-jnp.where(self.blk[8], low, 0.0) for low in lows]
        x = [self.eye + a for a in nb]
        pw = [self.dot(a, a) for a in nb]
        st = [self.dot(jnp.concatenate([xa, p], axis=0), p) for xa, p in zip(x, pw)]
        x = [xa + s[:c] for xa, s in zip(x, st)]
        x = [xa + self.dot(xa, s[c:]) for xa, s in zip(x, st)]
        for b in (8, 16, 32):
            inner = self.blk[b]
            outer = self.blk[2 * b] if 2 * b in self.blk else None
            off = jnp.logical_not(inner) if outer is None else jnp.logical_and(outer, jnp.logical_not(inner))
            xc = [self.dot(xa, jnp.where(off, low, 0.0)) for xa, low in zip(x, lows)]
            x = [xa - self.dot(t, xa) for xa, t in zip(x, xc)]
        return x


def _spread_heads(x, first):
    c = x.shape[0]
    low_half = lax.broadcasted_iota(jnp.int32, (c, LANES), 1) < HEAD_DIM
    tiles = []
    for j in range(0, DN_HEADS, 2):
        a = jnp.broadcast_to(x[:, first + j:first + j + 1], (c, LANES))
        b = jnp.broadcast_to(x[:, first + j + 1:first + j + 2], (c, LANES))
        tiles.append(jnp.where(low_half, a, b))
    return jnp.concatenate(tiles, axis=-1)


def _dn_chunk_terms(pk, dirs, q, k, v, gates):
    c = DN_CHUNK
    incl = [pk.incl[d] for d in dirs]
    strict = [pk.strict[d] for d in dirs]
    gc = [sum(_dot(pk.order[d], p) for p in _split3(g)) for d, g in zip(dirs, gates)]
    gcx = [_spread_heads(a, d * DN_HEADS) for d, a in zip(dirs, gc)]
    bx = [_spread_heads(g, (2 + d) * DN_HEADS) for d, g in zip(dirs, gates)]
    gtot = [a[c - 1:c, :] if d == 0 else a[0:1, :] for d, a in zip(dirs, gcx)]
    grow = [jnp.sum(pk.eye * a, 0, keepdims=True) for a in gcx]
    decay = [jnp.where(m, jnp.exp(jnp.minimum(a - r, 0.0)), 0.0) for m, a, r in zip(incl, gcx, grow)]
    kbeta = [a * b for a, b in zip(k, bx)]
    kq = [pk.dot_nt(jnp.concatenate([a, b], axis=0), kk) for a, b, kk in zip(kbeta, q, k)]
    low = [jnp.where(m, a[:c] * dc, 0.0) for m, a, dc in zip(strict, kq, decay)]
    intra = [jnp.where(m, a[c:] * dc, 0.0) for m, a, dc in zip(incl, kq, decay)]
    tinv = pk.tri_inverse(low)
    eg = [jnp.exp(a) for a in gcx]
    u = [pk.dot(t, a * b) for t, a, b in zip(tinv, v, bx)]
    wm = [pk.dot(t, a * e) for t, a, e in zip(tinv, kbeta, eg)]
    lhs = [jnp.concatenate([a * e, w_], axis=0).astype(BF16) for a, e, w_ in zip(q, eg, wm)]
    k_dec = [a * jnp.exp(gt - gx) for a, gt, gx in zip(k, gtot, gcx)]
    egl = [jnp.exp(gt) for gt in gtot]
    return [dict(lhs=a, u=b, intra=i, k_dec=kd, egl=e) for a, b, i, kd, e in zip(lhs, u, intra, k_dec, egl)]


def _dn_scan_kernel(xf, gf, xb, gb, of, ob, sa_ref, sb_ref, *, nsub):
    c, w = DN_CHUNK, DN_WIDTH

    @pl.when(pl.program_id(1) == 0)
    def _():
        sa_ref[...] = jnp.zeros_like(sa_ref)
        sb_ref[...] = jnp.zeros_like(sb_ref)

    pk = _Packed()
    refs = ((xf, gf, of), (xb, gb, ob))
    items = [(d, j if d == 0 else nsub - 1 - j) for j in range(nsub) for d in (0, 1)]
    rows = [slice(j * c, (j + 1) * c) for _, j in items]
    dirs = [d for d, _ in items]
    terms = _dn_chunk_terms(
        pk, dirs,
        [refs[d][0][0, r, 0:w] for d, r in zip(dirs, rows)],
        [refs[d][0][0, r, w:2 * w] for d, r in zip(dirs, rows)],
        [refs[d][0][0, r, 2 * w:3 * w] for d, r in zip(dirs, rows)],
        [refs[d][1][0, r, :] for d, r in zip(dirs, rows)])
    state = [(sa_ref[d], sb_ref[d]) for d in (0, 1)]
    for d, r, t in zip(dirs, rows, terms):
        sa, sb = state[d]
        lhs = t["lhs"]
        res = jnp.concatenate([_dot(lhs[:, :TILE_A], sa.astype(BF16)), _dot(lhs[:, TILE_A:], sb.astype(BF16))],
                              axis=-1)
        v_new = t["u"] - res[c:]
        refs[d][2][0, r, :] = res[:c] + pk.dot(t["intra"], v_new)
        ta, tb = pk.dot_tn(t["k_dec"], v_new)
        state[d] = (sa * t["egl"][:, :TILE_A] + ta, sb * t["egl"][:, TILE_A:] + tb)
    for d in (0, 1):
        sa_ref[d], sb_ref[d] = state[d]


DN_BLOCKS = (4 * DN_CHUNK, 2 * DN_CHUNK)


def _dn_scan(qkv, g, tc):
    bsz, t, w3 = qkv.shape
    w = w3 // 3
    c = next(blk for blk in DN_BLOCKS if tc % blk == 0 and t % blk == 0)
    nct, ncx = t // c, tc // c

    def bwd_blk(s):
        return jnp.where(s < ncx, ncx - 1 - s, nct - 1 - s + ncx)

    fspec = lambda width: pl.BlockSpec((1, c, width), lambda b, s: (b, s, 0))
    bspec = lambda width: pl.BlockSpec((1, c, width), lambda b, s: (b, bwd_blk(s), 0))
    return pl.pallas_call(
        functools.partial(_dn_scan_kernel, nsub=c // DN_CHUNK),
        grid=(bsz, nct),
        in_specs=[fspec(w3), fspec(LANES), bspec(w3), bspec(LANES)],
        out_specs=[fspec(w), bspec(w)],
        out_shape=[jax.ShapeDtypeStruct((bsz, t, w), F32)] * 2,
        scratch_shapes=[pltpu.VMEM((2, TILE_A, TILE_A), F32), pltpu.VMEM((2, w - TILE_A, w - TILE_A), F32)],
        compiler_params=_params(2),
        name="dn_scan",
    )(qkv, g, qkv, g)


def _post_kernel(*refs, alpha, f_chunk, nsplit, nsrc, ncb):
    x_refs, rest = refs[:nsrc], refs[nsrc:]
    (ya_ref, yb0_ref, yb1_ref, of_ref, ob_ref, z_ref, ngain_ref, nred_ref, nexp_ref,
     g1_ref, sh2_ref, sc2_ref, g2_ref, wa_ref, wb0_ref, wb1_ref, wc_ref, l1g_ref, l1b_ref, l2g_ref, l2b_ref,
     wup_ref, wdn_ref, o_ref) = rest
    tm = o_ref.shape[1]
    groups = [slice(r * tm // nsplit, (r + 1) * tm // nsplit) for r in range(nsplit)]
    is_ctx = pl.program_id(1) < ncb

    def x_rows(r):
        return x_refs[0][0, r, :] if nsrc == 1 else jnp.where(is_ctx, x_refs[0][0, r, :], x_refs[1][0, r, :])

    o = [of_ref[0, r, :] + ob_ref[0, r, :] for r in groups]
    yc = [(a * _head_rsqrt(a, nred_ref, nexp_ref) * ngain_ref[...]
           * _silu(z_ref[0, r, :].astype(F32))).astype(BF16) for a, r in zip(o, groups)]
    branch = [(_dot(ya_ref[0, r, :], wa_ref[...]) + _dot(yb0_ref[0, 0, r, :], wb0_ref[...])
               + _dot(yb1_ref[0, 0, r, :], wb1_ref[...]) + _dot(c, wc_ref[...])) for c, r in zip(yc, groups)]
    x1 = [_norm(alpha * x_rows(r) + g1_ref[...] * br) * l1g_ref[...] + l1b_ref[...]
          for br, r in zip(branch, groups)]
    h = [(_norm(a) * (1.0 + sc2_ref[...]) + sh2_ref[...]).astype(BF16) for a in x1]
    d_ff = wup_ref.shape[1]
    m = [None] * nsplit
    for f in range(0, d_ff, f_chunk):
        up = [jnp.maximum(_dot(a, wup_ref[:, f:f + f_chunk]), 0.0) for a in h]
        part = [_dot((a * a).astype(BF16), wdn_ref[f:f + f_chunk, :]) for a in up]
        m = [p if acc is None else acc + p for acc, p in zip(m, part)]
    for r, a, mm in zip(groups, x1, m):
        o_ref[0, r, :] = _norm(alpha * a + g2_ref[...] * mm) * l2g_ref[...] + l2b_ref[...]


def _post(src, ya, yb, o_f, o_b, pz, norm_g, modl, w_out, ln1_g, ln1_b, ln2_g, ln2_b, w_up, w_down,
          tm, ncb, ctx_row, alpha, need_ctx):
    parts = src if isinstance(src, tuple) else (src,)
    bsz, _, d = parts[0].shape
    t = sum(p.shape[1] for p in parts)
    assert need_ctx or len(parts) == 1
    ngain = jnp.tile(norm_g, DN_HEADS).reshape(1, DN_WIDTH)
    nred, nexp = _head_maps(DN_WIDTH, 1.0 / HEAD_DIM)
    first = 0 if need_ctx else ncb
    yb_first = first - (t - yb.shape[2]) // tm
    gw = ATT_GROUP * HEAD_DIM
    o1 = A_WIDTH
    wa, wb0, wb1, wc = w_out[:o1], w_out[o1:o1 + gw], w_out[o1 + gw:o1 + 2 * gw], w_out[o1 + 2 * gw:]
    mspec = lambda c: pl.BlockSpec(
        (None, None, 1, d), lambda b, i: (jnp.where(i + first < ncb, ctx_row, b), c, 0, 0))
    rspec = lambda w: pl.BlockSpec((1, tm, w), lambda b, i: (b, i + first, 0))
    vec = lambda a: a.reshape(1, d)
    wspec = lambda a: pl.BlockSpec(a.shape, lambda b, i: (0, 0), pipeline_mode=pl.Buffered(1))
    weights = [w.astype(BF16) for w in (wa, wb0, wb1, wc)]
    w_up, w_down = w_up.astype(BF16), w_down.astype(BF16)
    x_arrays, x_specs = _row_block_sources(src, tm, ncb, first=first)
    return pl.pallas_call(
        functools.partial(_post_kernel, alpha=alpha, f_chunk=1024, nsplit=2, nsrc=len(parts), ncb=ncb),
        grid=(bsz, t // tm - first),
        in_specs=x_specs + [
            rspec(A_WIDTH),
            pl.BlockSpec((1, 1, tm, gw), lambda b, i: (b, 0, i + yb_first, 0)),
            pl.BlockSpec((1, 1, tm, gw), lambda b, i: (b, 1, i + yb_first, 0)),
            rspec(DN_WIDTH), rspec(DN_WIDTH), rspec(DN_WIDTH),
            _const_spec(ngain.shape), _const_spec(nred.shape), _const_spec(nexp.shape),
            mspec(2), mspec(3), mspec(4), mspec(5),
            *[wspec(w) for w in weights],
            _const_spec((1, d)), _const_spec((1, d)), _const_spec((1, d)), _const_spec((1, d)),
            wspec(w_up), wspec(w_down),
        ],
        out_specs=pl.BlockSpec((1, tm, d), lambda b, i: (b, i, 0)),
        out_shape=jax.ShapeDtypeStruct((bsz, t - first * tm, d), F32),
        compiler_params=_params(2),
        name="post_mixer",
    )(*x_arrays, ya, yb, yb, o_f, o_b, pz, ngain, nred, nexp, modl, modl, modl, modl, *weights,
      vec(ln1_g), vec(ln1_b), vec(ln2_g), vec(ln2_b), w_up, w_down)


def _rope_tables(tc, tl):
    pos = jnp.arange(tl)
    row = (pos // GRID_W).astype(F32)
    col = (pos % GRID_W).astype(F32)
    half = HEAD_DIM // 2
    inv = 1.0 / (ROPE_THETA ** (jnp.arange(0, half, 2, dtype=F32) / half))
    ar, ac = row[:, None] * inv, col[:, None] * inv
    cos = jnp.concatenate([jnp.cos(ar), jnp.cos(ar), jnp.cos(ac), jnp.cos(ac)], -1)
    sin = jnp.concatenate([-jnp.sin(ar), jnp.sin(ar), -jnp.sin(ac), jnp.sin(ac)], -1)
    cos = jnp.concatenate([jnp.ones((tc, HEAD_DIM), F32), cos], 0)
    sin = jnp.concatenate([jnp.zeros((tc, HEAD_DIM), F32), sin], 0)
    scale = HEAD_DIM ** -0.5 * math.log2(math.e)
    tile =lambda a: jnp.concatenate([jnp.tile(a, (1, ATT_Q_HEADS)) * scale, jnp.tile(a, (1, ATT_KV_HEADS))], -1)
    return tile(cos), tile(sin)


def kernel(x, c, ctx, c_ctx, mod_w, mod_b, w_in, w_out, gmlp_ln_g, gmlp_ln_b, gmlp_w_s, gmlp_b_s,
           attn_q_g, attn_k_g, dn_conv_w, dn_a_log, dn_dt_bias, dn_norm_g,
           ln1_g, ln1_b, ln2_g, ln2_b, w_up, w_down):
    bsz, tl, d = x.shape
    tc = ctx.shape[1]
    depth = mod_w.shape[0]
    assert bsz < MOD_ROWS and tl % GRID_W == 0 and tc % MLP_CHUNK == 0 and tl % MLP_CHUNK == 0
    tm = 256 if tc % 256 == 0 else MLP_CHUNK
    ncb = tc // tm
    ctx_row = bsz
    alpha = (2 * depth) ** 0.25

    cs = jnp.zeros((MOD_ROWS, d), F32).at[:bsz].set(c).at[ctx_row].set(c_ctx)
    mod = _modulation(cs, mod_w, mod_b).reshape(depth, MOD_ROWS, 6, 1, d)
    cos_t, sin_t = _rope_tables(tc, tl)
    xcat = (ctx, x) if depth > 1 else jnp.concatenate([ctx, x], axis=1)

    for l in range(depth):
        need_ctx = l < depth - 1
        w_in_p = jnp.pad(w_in[l], ((0, 0), (0, IN_COLS_PAD - IN_COLS))).astype(BF16)
        ya, qa, ka, va, dqkv, dg, pz = _pre_mixer(
            xcat, mod[l], w_in_p, gmlp_ln_g[l], gmlp_ln_b[l], gmlp_w_s[l], gmlp_b_s[l], cos_t, sin_t,
            attn_q_g[l], attn_k_g[l], dn_conv_w[l], dn_a_log[l], dn_dt_bias[l], tm, ncb, ctx_row)
        yb_ctx, yb = _attention(qa, ka, va, tc, need_ctx)
        if need_ctx:
            yb = jnp.concatenate([yb_ctx, yb], axis=2)
        o_f, o_b = _dn_scan(dqkv, dg, tc)
        xcat = _post(xcat, ya, yb, o_f, o_b, pz, dn_norm_g[l], mod[l], w_out[l], ln1_g[l], ln1_b[l], ln2_g[l], ln2_b[l],
                     w_up[l], w_down[l], tm, ncb, ctx_row, alpha, need_ctx)
    return xcat
```

```python
import functools
import math

import jax
import jax.numpy as jnp
from jax import lax
from jax.experimental import pallas as pl
from jax.experimental.pallas import tpu as pltpu

F32 = jnp.float32
BF16 = jnp.bfloat16

GRID_W = 64
HEAD_DIM = 64
A_GROUPS = 4
A_WIDTH = A_GROUPS * HEAD_DIM
MLP_CHUNK = 128
ATT_Q_HEADS = 6
ATT_KV_HEADS = 2
ATT_GROUP = ATT_Q_HEADS // ATT_KV_HEADS
ATT_WIDTH = ATT_Q_HEADS * HEAD_DIM
KV_WIDTH = ATT_KV_HEADS * HEAD_DIM
ROPE_THETA = 10000.0
DN_HEADS = 6
DN_WIDTH = DN_HEADS * HEAD_DIM
DN_CONV = 5
DN_CHUNK = 64
A_COLS = 2 * A_WIDTH
B_COLS = ATT_WIDTH + 2 * KV_WIDTH
QKV_COLS = 3 * DN_WIDTH
GATE_COLS = 4 * DN_HEADS
IN_COLS = A_COLS + B_COLS + QKV_COLS + DN_WIDTH + GATE_COLS
LANES = 128
SUBLANES = 8
IN_COLS_PAD = IN_COLS - GATE_COLS + LANES
EPS = 1e-6
MOD_ROWS = 8
VMEM_LIMIT = 56 * 1024 * 1024


def _dot(a, b):
    return jnp.dot(a, b, preferred_element_type=F32)


def _dot_nt(a, b):
    return lax.dot_general(a, b, (((1,), (1,)), ((), ())), preferred_element_type=F32)


def _dot_tn(a, b):
    return lax.dot_general(a, b, (((0,), (0,)), ((), ())), preferred_element_type=F32)


def _split(x):
    hi = x.astype(BF16)
    lo = (x - hi.astype(F32)).astype(BF16)
    return hi, lo


def _dot_x3(a, b):
    ah, al = _split(a)
    bh, bl = _split(b)
    return _dot(ah, bh) + (_dot(ah, bl) + _dot(al, bh))


def _dot_lhs_split(x, m):
    hi, lo = _split(x)
    return _dot(hi, m) + _dot(lo, m)


def _norm(x):
    mu = jnp.mean(x, -1, keepdims=True)
    xc = x - mu
    var = jnp.mean(xc * xc, -1, keepdims=True)
    return xc * lax.rsqrt(var + EPS)


def _sigmoid(x):
    return 1.0 / (1.0 + jnp.exp(-x))


def _silu(x):
    return x * _sigmoid(x)


def _params(n_grid, vmem=VMEM_LIMIT):
    return pltpu.CompilerParams(dimension_semantics=("arbitrary",) * n_grid, vmem_limit_bytes=vmem)


def _const_spec(shape):
    nd = len(shape)
    return pl.BlockSpec(shape, lambda *_: (0,) * nd)


def _mod_kernel(cs_ref, w_ref, b_ref, o_ref):
    cs = cs_ref[...]
    o_ref[0] = _dot_x3(_silu(cs), w_ref[0]) + b_ref[0]


def _modulation(cs, mod_w, mod_b):
    depth, d, n = mod_w.shape
    tn = 1536
    return pl.pallas_call(
        _mod_kernel,
        grid=(depth, n // tn),
        in_specs=[
            pl.BlockSpec((MOD_ROWS, d), lambda l, j: (0, 0)),
            pl.BlockSpec((1, d, tn), lambda l, j: (l, 0, j)),
            pl.BlockSpec((1, 1, tn), lambda l, j: (l, 0, j)),
        ],
        out_specs=pl.BlockSpec((1, MOD_ROWS, tn), lambda l, j: (l, 0, j)),
        out_shape=jax.ShapeDtypeStruct((depth, MOD_ROWS, n), F32),
        compiler_params=_params(2),
        name="modulation",
    )(cs, mod_w, mod_b.reshape(depth, 1, n))


def _gmlp_rows(pa, lng_ref, lnb_ref, ws_ref, bias_ref, o_ref, r0):
    a = 0.5 * pa * (1.0 + lax.erf(pa * (2.0 ** -0.5)))
    u = a[:, :A_WIDTH]
    v = _norm(a[:, A_WIDTH:]) * lng_ref[...] + lnb_ref[...]
    vb = v.astype(BF16)
    mixed = jnp.concatenate(
        [_dot(ws_ref[g], vb[:, g * HEAD_DIM:(g + 1) * HEAD_DIM]) for g in range(A_GROUPS)], axis=-1)
    o_ref[0, r0:r0 + MLP_CHUNK, :] = (u * (mixed + bias_ref[...])).astype(BF16)


def _swap_rope_pairs(x):
    n = x.shape[-1]
    lane = lax.broadcasted_iota(jnp.int32, x.shape, x.ndim - 1)
    first = (lane & 31) < 16
    return jnp.where(first, pltpu.roll(x, n - 16, x.ndim - 1), pltpu.roll(x, 16, x.ndim - 1))


def _head_rsqrt(x, red_ref, exp_ref):
    s = _dot_lhs_split(x * x, red_ref[...])
    return _dot_lhs_split(lax.rsqrt(s + EPS), exp_ref[...])


def _head_maps(width, value):
    head = jnp.arange(width) // HEAD_DIM
    col = jnp.arange(LANES)
    red = jnp.where(head[:, None] == col[None, :], value, 0.0).astype(BF16)
    return red, (col[:, None] == head[None, :]).astype(BF16)


def _attn_prep_rows(pb, cos_ref, sin_ref, gain_ref, red_ref, exp_ref, q_out, k_out, v_out, r0):
    nqk = ATT_WIDTH + KV_WIDTH
    rows = slice(r0, r0 + pb.shape[0])
    qk = pb[:, :nqk]
    qk = qk * _head_rsqrt(qk, red_ref, exp_ref) * gain_ref[...]
    r = qk * cos_ref[rows, :] + _swap_rope_pairs(qk) * sin_ref[rows, :]
    for h in range(ATT_Q_HEADS):
        q_out[0, h, rows, :] = r[:, h * HEAD_DIM:(h + 1) * HEAD_DIM].astype(BF16)
    vv = pb[:, nqk:nqk + KV_WIDTH]
    lane = lax.broadcasted_iota(jnp.int32, vv.shape, 1)
    for h in range(ATT_KV_HEADS):
        k_out[0, h, rows, :] = r[:, ATT_WIDTH + h * HEAD_DIM:ATT_WIDTH + (h + 1) * HEAD_DIM].astype(BF16)
        vh = vv if h == 0 else pltpu.roll(vv, KV_WIDTH - h * HEAD_DIM, 1)
        v_out[0, h, rows, :] = jnp.where(lane < HEAD_DIM, vh, jnp.where(lane == HEAD_DIM, 1.0, 0.0)).astype(BF16)


KV_BLOCK = 256


def _key_blocks(nkeys):
    return [(s0, min(KV_BLOCK, nkeys - s0)) for s0 in range(0, nkeys, KV_BLOCK)]


def _store_heads(o_ref, acc, tq):
    o = acc[:, :HEAD_DIM] / acc[:, HEAD_DIM:HEAD_DIM + 1]
    for g in range(ATT_GROUP):
        o_ref[0, 0, :, g * HEAD_DIM:(g + 1) * HEAD_DIM] = o[g * tq:(g + 1) * tq].astype(BF16)


def _attn_lat_kernel(qe_ref, qo_ref, k_ref, v_ref, oe_ref, oo_ref, sa_ref, sb_ref, ma_ref, mb_ref, *, tq):
    t = k_ref.shape[2]

    @pl.when(pl.program_id(2) == 0)
    def _():
        sb_ref[...] = jnp.zeros(sb_ref.shape, F32)
        mb_ref[...] = jnp.zeros(mb_ref.shape, F32)

    def half(q, s_new, m_new, s_old, m_old, o_ref):
        m = jnp.max(m_old[...], -1, keepdims=True)
        mx = None
        acc = None
        for s0, n in _key_blocks(t):
            s = _dot_nt(q, k_ref[0, 0, s0:s0 + n, :])
            s_new[:, s0:s0 + n] = s
            for l0 in range(0, n, LANES):
                part = s[:, l0:l0 + LANES]
                mx = part if mx is None else jnp.maximum(mx, part)
            p = jnp.exp2(s_old[:, s0:s0 + n] - m).astype(BF16)
            part = _dot(p, v_ref[0, 0, s0:s0 + n, :])
            acc = part if acc is None else acc + part
        m_new[...] = mx
        _store_heads(o_ref, acc, tq)

    rows = ATT_GROUP * tq
    half(qe_ref[0].reshape(rows, HEAD_DIM), sa_ref, ma_ref, sb_ref, mb_ref, oo_ref)
    half(qo_ref[0].reshape(rows, HEAD_DIM), sb_ref, mb_ref, sa_ref, ma_ref, oe_ref)


def _attn_ctx_kernel(q_ref, k_ref, v_ref, o_ref, *, tc):
    q = q_ref[0].reshape(ATT_GROUP * tc, HEAD_DIM)
    s = _dot_nt(q, k_ref[0, 0])
    p = jnp.exp2(s - jnp.max(s, -1, keepdims=True)).astype(BF16)
    _store_heads(o_ref, _dot(p, v_ref[0, 0]), tc)


def _attention(q, k, v, tc, need_ctx, tq=256):
    bsz, _, t, _ = q.shape
    gw = ATT_GROUP * HEAD_DIM
    assert tc % tq == 0 and (t - tc) % (2 * tq) == 0
    first = tc // tq
    n2 = (t - tc) // (2 * tq)
    rows = ATT_GROUP * tq
    half_shape = jax.ShapeDtypeStruct((bsz, ATT_KV_HEADS, n2 * tq, gw), BF16)
    y_even, y_odd = pl.pallas_call(
        functools.partial(_attn_lat_kernel, tq=tq),
        grid=(bsz, ATT_KV_HEADS, n2 + 1),
        in_specs=[
            pl.BlockSpec((1, ATT_GROUP, tq, HEAD_DIM),
                         lambda b, j, i: (b, j, first + 2 * jnp.minimum(i, n2 - 1), 0)),
            pl.BlockSpec((1, ATT_GROUP, tq, HEAD_DIM),
                         lambda b, j, i: (b, j, first + 2 * jnp.minimum(i, n2 - 1) + 1, 0)),
            pl.BlockSpec((1, 1, t, HEAD_DIM), lambda b, j, i: (b, j, 0, 0)),
            pl.BlockSpec((1, 1, t, LANES), lambda b, j, i: (b, j, 0, 0)),
        ],
        out_specs=[pl.BlockSpec((1, 1, tq, gw), lambda b, j, i: (b, j, jnp.minimum(i, n2 - 1), 0)),
                   pl.BlockSpec((1, 1, tq, gw), lambda b, j, i: (b, j, jnp.maximum(i - 1, 0), 0))],
        out_shape=[half_shape, half_shape],
        scratch_shapes=[pltpu.VMEM((rows, t), F32), pltpu.VMEM((rows, t), F32),
                        pltpu.VMEM((rows, LANES), F32), pltpu.VMEM((rows, LANES), F32)],
        compiler_params=_params(3),
        name="attention",
    )(q, q, k, v)
    y_lat = jnp.stack([y_even.reshape(bsz, ATT_KV_HEADS, n2, tq, gw),
                       y_odd.reshape(bsz, ATT_KV_HEADS, n2, tq, gw)], axis=3).reshape(bsz, ATT_KV_HEADS, t - tc, gw)
    if not need_ctx:
        return None, y_lat
    y_ctx = pl.pallas_call(
        functools.partial(_attn_ctx_kernel, tc=tc),
        grid=(bsz, ATT_KV_HEADS),
        in_specs=[
            pl.BlockSpec((1, ATT_GROUP, tc, HEAD_DIM), lambda b, j: (b, j, 0, 0)),
            pl.BlockSpec((1, 1, tc, HEAD_DIM), lambda b, j: (b, j, 0, 0)),
            pl.BlockSpec((1, 1, tc, LANES), lambda b, j: (b, j, 0, 0)),
        ],
        out_specs=pl.BlockSpec((1, 1, tc, gw), lambda b, j: (b, j, 0, 0)),
        out_shape=jax.ShapeDtypeStruct((bsz, ATT_KV_HEADS, tc, gw), BF16),
        compiler_params=_params(2),
        name="attention_ctx",
    )(q, k, v)
    return y_ctx, y_lat


HALO = SUBLANES


def _dn_prep_rows(xe_ref, pg, cw_ref, red_ref, exp_ref, al_ref, dt_ref, qkv_out, g_out, r0):
    n = pg.shape[0]
    rows = slice(r0, r0 + n)
    first = HALO - DN_CONV // 2
    acc = cw_ref[0:1, :] * xe_ref[first:first + n, :]
    for j in range(1, DN_CONV):
        acc = acc + cw_ref[j:j + 1, :] * xe_ref[first + j:first + j + n, :]
    y = _silu(acc)
    qk = y[:, :2 * DN_WIDTH]
    qk = qk * _head_rsqrt(qk, red_ref, exp_ref)
    qkv_out[0, rows, :DN_WIDTH] = qk[:, :DN_WIDTH] * (HEAD_DIM ** -0.5)
    qkv_out[0, rows, DN_WIDTH:2 * DN_WIDTH] = qk[:, DN_WIDTH:]
    qkv_out[0, rows, 2 * DN_WIDTH:] = y[:, 2 * DN_WIDTH:]
    z = pg + dt_ref[...]
    softplus = jnp.maximum(z, 0.0) + jnp.log1p(jnp.exp(-jnp.abs(z)))
    lane = lax.broadcasted_iota(jnp.int32, pg.shape, 1)
    g_out[0, rows, :] = jnp.where(lane < 2 * DN_HEADS, -jnp.exp(al_ref[...]) * softplus, _sigmoid(pg))


def _pre_mixer_kernel(*refs, tm, ncb, nblk, nsrc):
    rows_refs, rest = refs[:3 * nsrc], refs[3 * nsrc:]
    (sh_ref, sc_ref, w_ref, lng_ref, lnb_ref, ws_ref, bias_ref,
     cos_ref, sin_ref, again_ref, ared_ref, aexp_ref, cw_ref, dred_ref, dexp_ref, al_ref, dt_ref,
     ya_out, q_out, k_out, v_out, dqkv_out, g_out, z_out, xe_ref) = rest
    i = pl.program_id(1)
    seg_start = jnp.logical_or(i == 0, i == ncb)
    seg_end = jnp.logical_or(i == ncb - 1, i == nblk - 1)
    cand = [jnp.concatenate([rows_refs[3 * s + 1][0], rows_refs[3 * s][0], rows_refs[3 * s + 2][0]], axis=0)
            for s in range(nsrc)]
    rows = cand[0] if nsrc == 1 else jnp.where(i < ncb, cand[0], cand[1])
    hf = _norm(rows) * (1.0 + sc_ref[...]) + sh_ref[...]
    o_b, o_q, o_z, o_g = A_COLS, A_COLS + B_COLS, A_COLS + B_COLS + QKV_COLS, IN_COLS - GATE_COLS
    n = MLP_CHUNK
    ngroups = tm // n

    def project(g):
        r0 = g * n
        h = hf[HALO + r0:HALO + r0 + n].astype(BF16)
        xe_ref[g] = _dot(hf[r0:r0 + n + 2 * HALO].astype(BF16), w_ref[:, o_q:o_z])
        if g == 0:
            xe_ref[g, 0:HALO, :] = jnp.where(seg_start, 0.0, xe_ref[g, 0:HALO, :])
        if g == ngroups - 1:
            xe_ref[g, HALO + n:2 * HALO + n, :] = jnp.where(seg_end, 0.0, xe_ref[g, HALO + n:2 * HALO + n, :])
        z_out[0, r0:r0 + n, :] = _dot(h, w_ref[:, o_z:o_g]).astype(BF16)
        return _dot(h, w_ref[:, :o_b]), _dot(h, w_ref[:, o_b:o_q]), _dot(h, w_ref[:, o_g:])

    def finish(g, pa, pb, pg):
        r0 = g * n
        _gmlp_rows(pa, lng_ref, lnb_ref, ws_ref, bias_ref, ya_out, r0)
        _attn_prep_rows(pb, cos_ref, sin_ref, again_ref, ared_ref, aexp_ref, q_out, k_out, v_out, r0)
        _dn_prep_rows(xe_ref.at[g], pg, cw_ref, dred_ref, dexp_ref, al_ref, dt_ref, dqkv_out, g_out, r0)

    pending = project(0)
    for g in range(1, ngroups):
        nxt = project(g)
        finish(g - 1, *pending)
        pending = nxt
    finish(ngroups - 1, *pending)


def _row_block_sources(src, tm, ncb, first=0, halo=False):
    parts = src if isinstance(src, tuple) else (src,)
    starts = (0, ncb) if len(parts) == 2 else (0,)
    hb = tm // HALO
    arrays, specs = [], []
    for arr, start in zip(parts, starts):
        d = arr.shape[-1]
        nb = arr.shape[1] // tm

        def blk(i, start=start, nb=nb):
            return jnp.clip(i + first - start, 0, nb - 1)

        arrays.append(arr)
        specs.append(pl.BlockSpec((1, tm, d), lambda b, i, blk=blk: (b, blk(i), 0)))
        if halo:
            arrays += [arr, arr]
            specs.append(pl.BlockSpec((1, HALO, d), lambda b, i, blk=blk: (b, jnp.maximum(blk(i) * hb - 1, 0), 0)))
            specs.append(pl.BlockSpec(
                (1, HALO, d), lambda b, i, blk=blk, nb=nb: (b, jnp.minimum((blk(i) + 1) * hb, nb * hb - 1), 0)))
    return arrays, specs


def _pre_mixer(src, modl, w_in_p, ln_g, ln_b, w_s, b_s, cos_t, sin_t, q_g, k_g, conv_w, a_log, dt_bias,
               tm, ncb, ctx_row):
    parts = src if isinstance(src, tuple) else (src,)
    bsz, _, d = parts[0].shape
    t = sum(p.shape[1] for p in parts)
    nblk = t // tm
    row_arrays, row_specs = _row_block_sources(src, tm, ncb, halo=True)
    nqk = ATT_WIDTH + KV_WIDTH
    bias = jnp.repeat(b_s.T, HEAD_DIM, axis=1)
    gain = jnp.concatenate([jnp.tile(q_g, ATT_Q_HEADS), jnp.tile(k_g, ATT_KV_HEADS)]).reshape(1, nqk)
    ared, aexp = _head_maps(nqk, 1.0 / HEAD_DIM)
    dred, dexp = _head_maps(2 * DN_WIDTH, 1.0)
    pad = lambda v: jnp.zeros((1, LANES), F32).at[0, :2 * DN_HEADS].set(v.reshape(-1))
    mspec = lambda c: pl.BlockSpec((None, None, 1, d), lambda b, i: (jnp.where(i < ncb, ctx_row, b), c, 0, 0))
    rspec = lambda w: pl.BlockSpec((1, tm, w), lambda b, i: (b, i, 0))
    hspec = lambda nh, w: pl.BlockSpec((1, nh, tm, w), lambda b, i: (b, 0, i, 0))
    heads = ((ATT_Q_HEADS, HEAD_DIM), (ATT_KV_HEADS, HEAD_DIM), (ATT_KV_HEADS, LANES))
    rows_out = ((QKV_COLS, F32), (LANES, F32), (DN_WIDTH, BF16))
    return pl.pallas_call(
        functools.partial(_pre_mixer_kernel, tm=tm, ncb=ncb, nblk=nblk, nsrc=len(parts)),
        grid=(bsz, nblk),
        in_specs=row_specs + [
            mspec(0), mspec(1),
            pl.BlockSpec(w_in_p.shape, lambda b, i: (0, 0), pipeline_mode=pl.Buffered(1)),
            _const_spec((1, A_WIDTH)), _const_spec((1, A_WIDTH)), _const_spec(w_s.shape), _const_spec(bias.shape),
            pl.BlockSpec((tm, nqk), lambda b, i: (i, 0)), pl.BlockSpec((tm, nqk), lambda b, i: (i, 0)),
            _const_spec((1, nqk)), _const_spec(ared.shape), _const_spec(aexp.shape),
            _const_spec(conv_w.shape), _const_spec(dred.shape), _const_spec(dexp.shape),
            _const_spec((1, LANES)), _const_spec((1, LANES)),
        ],
        out_specs=[rspec(A_WIDTH)] + [hspec(nh, w) for nh, w in heads] + [rspec(w) for w, _ in rows_out],
        out_shape=[jax.ShapeDtypeStruct((bsz, t, A_WIDTH), BF16)]
        + [jax.ShapeDtypeStruct((bsz, nh, t, w), BF16) for nh, w in heads]
        + [jax.ShapeDtypeStruct((bsz, t, w), dt) for w, dt in rows_out],
        scratch_shapes=[pltpu.VMEM((tm // MLP_CHUNK, MLP_CHUNK + 2 * HALO, QKV_COLS), F32)],
        compiler_params=_params(2),
        name="pre_mixer",
    )(*row_arrays, modl, modl, w_in_p,
      ln_g.reshape(1, -1), ln_b.reshape(1, -1), w_s.astype(BF16), bias,
      cos_t, sin_t, gain, ared, aexp, conv_w, dred, dexp, pad(a_log), pad(dt_bias))


TILE_A = 4 * HEAD_DIM


def _split3(x):
    hi = x.astype(BF16)
    r = x - hi.astype(F32)
    mid = r.astype(BF16)
    lo = (r - mid.astype(F32)).astype(BF16)
    return hi, mid, lo


def _head_id(shape, axis):
    return jnp.right_shift(lax.broadcasted_iota(jnp.int32, shape, axis), 6)


class _Packed:
    def __init__(self):
        c, w = DN_CHUNK, DN_WIDTH
        wb = w - TILE_A
        self.mask_a = _head_id((TILE_A, TILE_A), 0) == _head_id((TILE_A, TILE_A), 1)
        self.mask_b = _head_id((wb, wb), 0) == _head_id((wb, wb), 1)
        self.ri = lax.broadcasted_iota(jnp.int32, (c, w), 0)
        self.cj = lax.broadcasted_iota(jnp.int32, (c, w), 1) & (HEAD_DIM - 1)
        self.eye = (self.ri == self.cj).astype(F32)
        self.incl = (self.cj <= self.ri, self.cj >= self.ri)
        self.strict = (self.cj < self.ri, self.cj > self.ri)
        r64 = lax.broadcasted_iota(jnp.int32, (c, c), 0)
        c64 = lax.broadcasted_iota(jnp.int32, (c, c), 1)
        self.order = ((c64 <= r64).astype(BF16), (c64 >= r64).astype(BF16))
        self.blk = {b: jnp.right_shift(self.ri, int(math.log2(b))) == jnp.right_shift(self.cj, int(math.log2(b)))
                    for b in (8, 16, 32)}

    def weights(self, y):
        y16 = y.astype(BF16)
        ya = jnp.concatenate([y16[:, :TILE_A]] * 4, axis=0)
        yb = jnp.concatenate([y16[:, TILE_A:]] * 2, axis=0)
        zero = jnp.zeros((), BF16)
        return jnp.where(self.mask_a, ya, zero), jnp.where(self.mask_b, yb, zero)

    def dot(self, x, y):
        wa, wb = self.weights(y)
        x16 = x.astype(BF16)
        return jnp.concatenate([_dot(x16[:, :TILE_A], wa), _dot(x16[:, TILE_A:], wb)], axis=-1)

    def dot_nt(self, x, y):
        wa, wb = self.weights(y)
        x16 = x.astype(BF16)
        return jnp.concatenate([_dot_nt(x16[:, :TILE_A], wa), _dot_nt(x16[:, TILE_A:], wb)], axis=-1)

    def dot_tn(self, x, y):
        x16, y16 = x.astype(BF16), y.astype(BF16)
        ta = _dot_tn(x16[:, :TILE_A], y16[:, :TILE_A])
        tb = _dot_tn(x16[:, TILE_A:], y16[:, TILE_A:])
        return jnp.where(self.mask_a, ta, 0.0), jnp.where(self.mask_b, tb, 0.0)

    def tri_inverse(self, lows):
        c = DN_CHUNK
        nb = [-jnp.where(self.blk[8], low, 0.0) for low in lows]
        x = [self.eye + a for a in nb]
        pw = [self.dot(a, a) for a in nb]
        st = [self.dot(jnp.concatenate([xa, p], axis=0), p) for xa, p in zip(x, pw)]
        x = [xa + s[:c] for xa, s in zip(x, st)]
        x = [xa + self.dot(xa, s[c:]) for xa, s in zip(x, st)]
        for b in (8, 16, 32):
            inner = self.blk[b]
            outer = self.blk[2 * b] if 2 * b in self.blk else None
            off = jnp.logical_not(inner) if outer is None else jnp.logical_and(outer, jnp.logical_not(inner))
            xc = [self.dot(xa, jnp.where(off, low, 0.0)) for xa, low in zip(x, lows)]
            x = [xa - self.dot(t, xa) for xa, t in zip(x, xc)]
        return x


def _spread_heads(x, first):
    c = x.shape[0]
    low_half = lax.broadcasted_iota(jnp.int32, (c, LANES), 1) < HEAD_DIM
    tiles = []
    for j in range(0, DN_HEADS, 2):
        a = jnp.broadcast_to(x[:, first + j:first + j + 1], (c, LANES))
        b = jnp.broadcast_to(x[:, first + j + 1:first + j + 2], (c, LANES))
        tiles.append(jnp.where(low_half, a, b))
    return jnp.concatenate(tiles, axis=-1)


def _dn_chunk_terms(pk, dirs, q, k, v, gates):
    c = DN_CHUNK
    incl = [pk.incl[d] for d in dirs]
    strict = [pk.strict[d] for d in dirs]
    gc = [sum(_dot(pk.order[d], p) for p in _split3(g)) for d, g in zip(dirs, gates)]
    gcx = [_spread_heads(a, d * DN_HEADS) for d, a in zip(dirs, gc)]
    bx = [_spread_heads(g, (2 + d) * DN_HEADS) for d, g in zip(dirs, gates)]
    gtot = [a[c - 1:c, :] if d == 0 else a[0:1, :] for d, a in zip(dirs, gcx)]
    grow = [jnp.sum(pk.eye * a, 0, keepdims=True) for a in gcx]
    decay = [jnp.where(m, jnp.exp(jnp.minimum(a - r, 0.0)), 0.0) for m, a, r in zip(incl, gcx, grow)]
    kbeta = [a * b for a, b in zip(k, bx)]
    kq = [pk.dot_nt(jnp.concatenate([a, b], axis=0), kk) for a, b, kk in zip(kbeta, q, k)]
    low = [jnp.where(m, a[:c] * dc, 0.0) for m, a, dc in zip(strict, kq, decay)]
    intra = [jnp.where(m, a[c:] * dc, 0.0) for m, a, dc in zip(incl, kq, decay)]
    tinv = pk.tri_inverse(low)
    eg = [jnp.exp(a) for a in gcx]
    u = [pk.dot(t, a * b) for t, a, b in zip(tinv, v, bx)]
    wm = [pk.dot(t, a * e) for t, a, e in zip(tinv, kbeta, eg)]
    lhs = [jnp.concatenate([a * e, w_], axis=0).astype(BF16) for a, e, w_ in zip(q, eg, wm)]
    k_dec = [a * jnp.exp(gt - gx) for a, gt, gx in zip(k, gtot, gcx)]
    egl = [jnp.exp(gt) for gt in gtot]
    return [dict(lhs=a, u=b, intra=i, k_dec=kd, egl=e) for a, b, i, kd, e in zip(lhs, u, intra, k_dec, egl)]


def _dn_scan_kernel(xf, gf, xb, gb, of, ob, sa_ref, sb_ref, *, nsub):
    c, w = DN_CHUNK, DN_WIDTH

    @pl.when(pl.program_id(1) == 0)
    def _():
        sa_ref[...] = jnp.zeros_like(sa_ref)
        sb_ref[...] = jnp.zeros_like(sb_ref)

    pk = _Packed()
    refs = ((xf, gf, of), (xb, gb, ob))
    items = [(d, j if d == 0 else nsub - 1 - j) for j in range(nsub) for d in (0, 1)]
    rows = [slice(j * c, (j + 1) * c) for _, j in items]
    dirs = [d for d, _ in items]
    terms = _dn_chunk_terms(
        pk, dirs,
        [refs[d][0][0, r, 0:w] for d, r in zip(dirs, rows)],
        [refs[d][0][0, r, w:2 * w] for d, r in zip(dirs, rows)],
        [refs[d][0][0, r, 2 * w:3 * w] for d, r in zip(dirs, rows)],
        [refs[d][1][0, r, :] for d, r in zip(dirs, rows)])
    state = [(sa_ref[d], sb_ref[d]) for d in (0, 1)]
    for d, r, t in zip(dirs, rows, terms):
        sa, sb = state[d]
        lhs = t["lhs"]
        res = jnp.concatenate([_dot(lhs[:, :TILE_A], sa.astype(BF16)), _dot(lhs[:, TILE_A:], sb.astype(BF16))],
                              axis=-1)
        v_new = t["u"] - res[c:]
        refs[d][2][0, r, :] = res[:c] + pk.dot(t["intra"], v_new)
        ta, tb = pk.dot_tn(t["k_dec"], v_new)
        state[d] = (sa * t["egl"][:, :TILE_A] + ta, sb * t["egl"][:, TILE_A:] + tb)
    for d in (0, 1):
        sa_ref[d], sb_ref[d] = state[d]


DN_BLOCKS = (4 * DN_CHUNK, 2 * DN_CHUNK)


def _dn_scan(qkv, g, tc):
    bsz, t, w3 = qkv.shape
    w = w3 // 3
    c = next(blk for blk in DN_BLOCKS if tc % blk == 0 and t % blk == 0)
    nct, ncx = t // c, tc // c

    def bwd_blk(s):
        return jnp.where(s < ncx, ncx - 1 - s, nct - 1 - s + ncx)

    fspec = lambda width: pl.BlockSpec((1, c, width), lambda b, s: (b, s, 0))
    bspec = lambda width: pl.BlockSpec((1, c, width), lambda b, s: (b, bwd_blk(s), 0))
    return pl.pallas_call(
        functools.partial(_dn_scan_kernel, nsub=c // DN_CHUNK),
        grid=(bsz, nct),
        in_specs=[fspec(w3), fspec(LANES), bspec(w3), bspec(LANES)],
        out_specs=[fspec(w), bspec(w)],
        out_shape=[jax.ShapeDtypeStruct((bsz, t, w), F32)] * 2,
        scratch_shapes=[pltpu.VMEM((2, TILE_A, TILE_A), F32), pltpu.VMEM((2, w - TILE_A, w - TILE_A), F32)],
        compiler_params=_params(2),
        name="dn_scan",
    )(qkv, g, qkv, g)


def _post_kernel(*refs, alpha, f_chunk, nsplit, nsrc, ncb):
    x_refs, rest = refs[:nsrc], refs[nsrc:]
    (ya_ref, yb0_ref, yb1_ref, of_ref, ob_ref, z_ref, ngain_ref, nred_ref, nexp_ref,
     g1_ref, sh2_ref, sc2_ref, g2_ref, wa_ref, wb0_ref, wb1_ref, wc_ref, l1g_ref, l1b_ref, l2g_ref, l2b_ref,
     wup_ref, wdn_ref, o_ref) = rest
    tm = o_ref.shape[1]
    groups = [slice(r * tm // nsplit, (r + 1) * tm // nsplit) for r in range(nsplit)]
    is_ctx = pl.program_id(1) < ncb

    def x_rows(r):
        return x_refs[0][0, r, :] if nsrc == 1 else jnp.where(is_ctx, x_refs[0][0, r, :], x_refs[1][0, r, :])

    o = [of_ref[0, r, :] + ob_ref[0, r, :] for r in groups]
    yc = [(a * _head_rsqrt(a, nred_ref, nexp_ref) * ngain_ref[...]
           * _silu(z_ref[0, r, :].astype(F32))).astype(BF16) for a, r in zip(o, groups)]
    branch = [(_dot(ya_ref[0, r, :], wa_ref[...]) + _dot(yb0_ref[0, 0, r, :], wb0_ref[...])
               + _dot(yb1_ref[0, 0, r, :], wb1_ref[...]) + _dot(c, wc_ref[...])) for c, r in zip(yc, groups)]
    x1 = [_norm(alpha * x_rows(r) + g1_ref[...] * br) * l1g_ref[...] + l1b_ref[...]
          for br, r in zip(branch, groups)]
    h = [(_norm(a) * (1.0 + sc2_ref[...]) + sh2_ref[...]).astype(BF16) for a in x1]
    d_ff = wup_ref.shape[1]
    m = [None] * nsplit
    for f in range(0, d_ff, f_chunk):
        up = [jnp.maximum(_dot(a, wup_ref[:, f:f + f_chunk]), 0.0) for a in h]
        part = [_dot((a * a).astype(BF16), wdn_ref[f:f + f_chunk, :]) for a in up]
        m = [p if acc is None else acc + p for acc, p in zip(m, part)]
    for r, a, mm in zip(groups, x1, m):
        o_ref[0, r, :] = _norm(alpha * a + g2_ref[...] * mm) * l2g_ref[...] + l2b_ref[...]


def _post(src, ya, yb, o_f, o_b, pz, norm_g, modl, w_out, ln1_g, ln1_b, ln2_g, ln2_b, w_up, w_down,
          tm, ncb, ctx_row, alpha, need_ctx):
    parts = src if isinstance(src, tuple) else (src,)
    bsz, _, d = parts[0].shape
    t = sum(p.shape[1] for p in parts)
    assert need_ctx or len(parts) == 1
    ngain = jnp.tile(norm_g, DN_HEADS).reshape(1, DN_WIDTH)
    nred, nexp = _head_maps(DN_WIDTH, 1.0 / HEAD_DIM)
    first = 0 if need_ctx else ncb
    yb_first = first - (t - yb.shape[2]) // tm
    gw = ATT_GROUP * HEAD_DIM
    o1 = A_WIDTH
    wa, wb0, wb1, wc = w_out[:o1], w_out[o1:o1 + gw], w_out[o1 + gw:o1 + 2 * gw], w_out[o1 + 2 * gw:]
    mspec = lambda c: pl.BlockSpec(
        (None, None, 1, d), lambda b, i: (jnp.where(i + first < ncb, ctx_row, b), c, 0, 0))
    rspec = lambda w: pl.BlockSpec((1, tm, w), lambda b, i: (b, i + first, 0))
    vec = lambda a: a.reshape(1, d)
    wspec = lambda a: pl.BlockSpec(a.shape, lambda b, i: (0, 0), pipeline_mode=pl.Buffered(1))
    weights = [w.astype(BF16) for w in (wa, wb0, wb1, wc)]
    w_up, w_down = w_up.astype(BF16), w_down.astype(BF16)
    x_arrays, x_specs = _row_block_sources(src, tm, ncb, first=first)
    return pl.pallas_call(
        functools.partial(_post_kernel, alpha=alpha, f_chunk=1024, nsplit=2, nsrc=len(parts), ncb=ncb),
        grid=(bsz, t // tm - first),
        in_specs=x_specs + [
            rspec(A_WIDTH),
            pl.BlockSpec((1, 1, tm, gw), lambda b, i: (b, 0, i + yb_first, 0)),
            pl.BlockSpec((1, 1, tm, gw), lambda b, i: (b, 1, i + yb_first, 0)),
            rspec(DN_WIDTH), rspec(DN_WIDTH), rspec(DN_WIDTH),
            _const_spec(ngain.shape), _const_spec(nred.shape), _const_spec(nexp.shape),
            mspec(2), mspec(3), mspec(4), mspec(5),
            *[wspec(w) for w in weights],
            _const_spec((1, d)), _const_spec((1, d)), _const_spec((1, d)), _const_spec((1, d)),
            wspec(w_up), wspec(w_down),
        ],
        out_specs=pl.BlockSpec((1, tm, d), lambda b, i: (b, i, 0)),
        out_shape=jax.ShapeDtypeStruct((bsz, t - first * tm, d), F32),
        compiler_params=_params(2),
        name="post_mixer",
    )(*x_arrays, ya, yb, yb, o_f, o_b, pz, ngain, nred, nexp, modl, modl, modl, modl, *weights,
      vec(ln1_g), vec(ln1_b), vec(ln2_g), vec(ln2_b), w_up, w_down)


def _rope_tables(tc, tl):
    pos = jnp.arange(tl)
    row = (pos // GRID_W).astype(F32)
    col = (pos % GRID_W).astype(F32)
    half = HEAD_DIM // 2
    inv = 1.0 / (ROPE_THETA ** (jnp.arange(0, half, 2, dtype=F32) / half))
    ar, ac = row[:, None] * inv, col[:, None] * inv
    cos = jnp.concatenate([jnp.cos(ar), jnp.cos(ar), jnp.cos(ac), jnp.cos(ac)], -1)
    sin = jnp.concatenate([-jnp.sin(ar), jnp.sin(ar), -jnp.sin(ac), jnp.sin(ac)], -1)
    cos = jnp.concatenate([jnp.ones((tc, HEAD_DIM), F32), cos], 0)
    sin = jnp.concatenate([jnp.zeros((tc, HEAD_DIM), F32), sin], 0)
    scale = HEAD_DIM ** -0.5 * math.log2(math.e)
    tile =lambda a: jnp.concatenate([jnp.tile(a, (1, ATT_Q_HEADS)) * scale, jnp.tile(a, (1, ATT_KV_HEADS))], -1)
    return tile(cos), tile(sin)


def kernel(x, c, ctx, c_ctx, mod_w, mod_b, w_in, w_out, gmlp_ln_g, gmlp_ln_b, gmlp_w_s, gmlp_b_s,
           attn_q_g, attn_k_g, dn_conv_w, dn_a_log, dn_dt_bias, dn_norm_g,
           ln1_g, ln1_b, ln2_g, ln2_b, w_up, w_down):
    bsz, tl, d = x.shape
    tc = ctx.shape[1]
    depth = mod_w.shape[0]
    assert bsz < MOD_ROWS and tl % GRID_W == 0 and tc % MLP_CHUNK == 0 and tl % MLP_CHUNK == 0
    tm = 256 if tc % 256 == 0 else MLP_CHUNK
    ncb = tc // tm
    ctx_row = bsz
    alpha = (2 * depth) ** 0.25

    cs = jnp.zeros((MOD_ROWS, d), F32).at[:bsz].set(c).at[ctx_row].set(c_ctx)
    mod = _modulation(cs, mod_w, mod_b).reshape(depth, MOD_ROWS, 6, 1, d)
    cos_t, sin_t = _rope_tables(tc, tl)
    xcat = (ctx, x) if depth > 1 else jnp.concatenate([ctx, x], axis=1)

    for l in range(depth):
        need_ctx = l < depth - 1
        w_in_p = jnp.pad(w_in[l], ((0, 0), (0, IN_COLS_PAD - IN_COLS))).astype(BF16)
        ya, qa, ka, va, dqkv, dg, pz = _pre_mixer(
            xcat, mod[l], w_in_p, gmlp_ln_g[l], gmlp_ln_b[l], gmlp_w_s[l], gmlp_b_s[l], cos_t, sin_t,
            attn_q_g[l], attn_k_g[l], dn_conv_w[l], dn_a_log[l], dn_dt_bias[l], tm, ncb, ctx_row)
        yb_ctx, yb = _attention(qa, ka, va, tc, need_ctx)
        if need_ctx:
            yb = jnp.concatenate([yb_ctx, yb], axis=2)
        o_f, o_b = _dn_scan(dqkv, dg, tc)
        xcat = _post(xcat, ya, yb, o_f, o_b, pz, dn_norm_g[l], mod[l], w_out[l], ln1_g[l], ln1_b[l], ln2_g[l], ln2_b[l],
                     w_up[l], w_down[l], tm, ncb, ctx_row, alpha, need_ctx)
    return xcat
```

```python
import functools
import math

import jax
import jax.numpy as jnp
from jax import lax
from jax.experimental import pallas as pl
from jax.experimental.pallas import tpu as pltpu

F32 = jnp.float32
BF16 = jnp.bfloat16

GRID_W = 64
HEAD_DIM = 64
A_GROUPS = 4
A_WIDTH = A_GROUPS * HEAD_DIM
MLP_CHUNK = 128
ATT_Q_HEADS = 6
ATT_KV_HEADS = 2
ATT_GROUP = ATT_Q_HEADS // ATT_KV_HEADS
ATT_WIDTH = ATT_Q_HEADS * HEAD_DIM
KV_WIDTH = ATT_KV_HEADS * HEAD_DIM
ROPE_THETA = 10000.0
DN_HEADS = 6
DN_WIDTH = DN_HEADS * HEAD_DIM
DN_CONV = 5
DN_CHUNK = 64
A_COLS = 2 * A_WIDTH
B_COLS = ATT_WIDTH + 2 * KV_WIDTH
QKV_COLS = 3 * DN_WIDTH
GATE_COLS = 4 * DN_HEADS
IN_COLS = A_COLS + B_COLS + QKV_COLS + DN_WIDTH + GATE_COLS
LANES = 128
SUBLANES = 8
IN_COLS_PAD = IN_COLS - GATE_COLS + LANES
EPS = 1e-6
MOD_ROWS = 8
VMEM_LIMIT = 56 * 1024 * 1024


def _dot(a, b):
    return jnp.dot(a, b, preferred_element_type=F32)


def _dot_nt(a, b):
    return lax.dot_general(a, b, (((1,), (1,)), ((), ())), preferred_element_type=F32)


def _dot_tn(a, b):
    return lax.dot_general(a, b, (((0,), (0,)), ((), ())), preferred_element_type=F32)


def _split(x):
    hi = x.astype(BF16)
    lo = (x - hi.astype(F32)).astype(BF16)
    return hi, lo


def _dot_x3(a, b):
    ah, al = _split(a)
    bh, bl = _split(b)
    return _dot(ah, bh) + (_dot(ah, bl) + _dot(al, bh))


def _dot_lhs_split(x, m):
    hi, lo = _split(x)
    return _dot(hi, m) + _dot(lo, m)


def _norm(x):
    mu = jnp.mean(x, -1, keepdims=True)
    xc = x - mu
    var = jnp.mean(xc * xc, -1, keepdims=True)
    return xc * lax.rsqrt(var + EPS)


def _sigmoid(x):
    return 1.0 / (1.0 + jnp.exp(-x))


def _silu(x):
    return x * _sigmoid(x)


def _params(n_grid, vmem=VMEM_LIMIT):
    return pltpu.CompilerParams(dimension_semantics=("arbitrary",) * n_grid, vmem_limit_bytes=vmem)


def _const_spec(shape):
    nd = len(shape)
    return pl.BlockSpec(shape, lambda *_: (0,) * nd)


def _mod_kernel(cs_ref, w_ref, b_ref, o_ref):
    cs = cs_ref[...]
    o_ref[0] = _dot_x3(_silu(cs), w_ref[0]) + b_ref[0]


def _modulation(cs, mod_w, mod_b):
    depth, d, n = mod_w.shape
    tn = 1536
    return pl.pallas_call(
        _mod_kernel,
        grid=(depth, n // tn),
        in_specs=[
            pl.BlockSpec((MOD_ROWS, d), lambda l, j: (0, 0)),
            pl.BlockSpec((1, d, tn), lambda l, j: (l, 0, j)),
            pl.BlockSpec((1, 1, tn), lambda l, j: (l, 0, j)),
        ],
        out_specs=pl.BlockSpec((1, MOD_ROWS, tn), lambda l, j: (l, 0, j)),
        out_shape=jax.ShapeDtypeStruct((depth, MOD_ROWS, n), F32),
        compiler_params=_params(2),
        name="modulation",
    )(cs, mod_w, mod_b.reshape(depth, 1, n))


def _gmlp_rows(pa, lng_ref, lnb_ref, ws_ref, bias_ref, o_ref, r0):
    a = 0.5 * pa * (1.0 + lax.erf(pa * (2.0 ** -0.5)))
    u = a[:, :A_WIDTH]
    v = _norm(a[:, A_WIDTH:]) * lng_ref[...] + lnb_ref[...]
    vb = v.astype(BF16)
    mixed = jnp.concatenate(
        [_dot(ws_ref[g], vb[:, g * HEAD_DIM:(g + 1) * HEAD_DIM]) for g in range(A_GROUPS)], axis=-1)
    o_ref[0, r0:r0 + MLP_CHUNK, :] = (u * (mixed + bias_ref[...])).astype(BF16)


def _swap_rope_pairs(x):
    n = x.shape[-1]
    lane = lax.broadcasted_iota(jnp.int32, x.shape, x.ndim - 1)
    first = (lane & 31) < 16
    return jnp.where(first, pltpu.roll(x, n - 16, x.ndim - 1), pltpu.roll(x, 16, x.ndim - 1))


def _head_rsqrt(x, red_ref, exp_ref):
    s = _dot_lhs_split(x * x, red_ref[...])
    return _dot_lhs_split(lax.rsqrt(s + EPS), exp_ref[...])


def _head_maps(width, value):
    head = jnp.arange(width) // HEAD_DIM
    col = jnp.arange(LANES)
    red = jnp.where(head[:, None] == col[None, :], value, 0.0).astype(BF16)
    return red, (col[:, None] == head[None, :]).astype(BF16)


def _attn_prep_rows(pb, cos_ref, sin_ref, gain_ref, red_ref, exp_ref, q_out, k_out, v_out, r0):
    nqk = ATT_WIDTH + KV_WIDTH
    rows = slice(r0, r0 + pb.shape[0])
    qk = pb[:, :nqk]
    qk = qk * _head_rsqrt(qk, red_ref, exp_ref) * gain_ref[...]
    r = qk * cos_ref[rows, :] + _swap_rope_pairs(qk) * sin_ref[rows, :]
    for h in range(ATT_Q_HEADS):
        q_out[0, h, rows, :] = r[:, h * HEAD_DIM:(h + 1) * HEAD_DIM].astype(BF16)
    vv = pb[:, nqk:nqk + KV_WIDTH]
    lane = lax.broadcasted_iota(jnp.int32, vv.shape, 1)
    for h in range(ATT_KV_HEADS):
        k_out[0, h, rows, :] = r[:, ATT_WIDTH + h * HEAD_DIM:ATT_WIDTH + (h + 1) * HEAD_DIM].astype(BF16)
        vh = vv if h == 0 else pltpu.roll(vv, KV_WIDTH - h * HEAD_DIM, 1)
        v_out[0, h, rows, :] = jnp.where(lane < HEAD_DIM, vh, jnp.where(lane == HEAD_DIM, 1.0, 0.0)).astype(BF16)


KV_BLOCK = 256


def _key_blocks(nkeys):
    return [(s0, min(KV_BLOCK, nkeys - s0)) for s0 in range(0, nkeys, KV_BLOCK)]


def _store_heads(o_ref, acc, tq):
    o = acc[:, :HEAD_DIM] / acc[:, HEAD_DIM:HEAD_DIM + 1]
    for g in range(ATT_GROUP):
        o_ref[0, 0, :, g * HEAD_DIM:(g + 1) * HEAD_DIM] = o[g * tq:(g + 1) * tq].astype(BF16)


def _attn_lat_kernel(qe_ref, qo_ref, k_ref, v_ref, oe_ref, oo_ref, sa_ref, sb_ref, ma_ref, mb_ref, *, tq):
    t = k_ref.shape[2]
    i = pl.program_id(2)
    last = pl.num_programs(2) - 1

    def half(q_ref, s_new, m_new, s_old, m_old, o_ref, scores=True, finish=True):
        if scores:
            q = q_ref[0].reshape(ATT_GROUP * tq, HEAD_DIM)
        if finish:
            m = jnp.max(m_old[...], -1, keepdims=True)
        mx = None
        acc = None
        for s0, n in _key_blocks(t):
            if scores:
                s = _dot_nt(q, k_ref[0, 0, s0:s0 + n, :])
                s_new[:, s0:s0 + n] = s
                for l0 in range(0, n, LANES):
                    part = s[:, l0:l0 + LANES]
                    mx = part if mx is None else jnp.maximum(mx, part)
            if finish:
                p = jnp.exp2(s_old[:, s0:s0 + n] - m).astype(BF16)
                part = _dot(p, v_ref[0, 0, s0:s0 + n, :])
                acc = part if acc is None else acc + part
        if scores:
            m_new[...] = mx
        if finish:
            _store_heads(o_ref, acc, tq)

    first_half = functools.partial(half, qe_ref, sa_ref, ma_ref, sb_ref, mb_ref, oo_ref)
    pl.when(i == 0)(functools.partial(first_half, finish=False))
    pl.when(jnp.logical_and(i > 0, i < last))(first_half)
    pl.when(i == last)(functools.partial(first_half, scores=False))
    pl.when(i < last)(functools.partial(half, qo_ref, sb_ref, mb_ref, sa_ref, ma_ref, oe_ref))


def _attn_ctx_kernel(q_ref, k_ref, v_ref, o_ref, *, tc):
    q = q_ref[0].reshape(ATT_GROUP * tc, HEAD_DIM)
    s = _dot_nt(q, k_ref[0, 0])
    p = jnp.exp2(s - jnp.max(s, -1, keepdims=True)).astype(BF16)
    _store_heads(o_ref, _dot(p, v_ref[0, 0]), tc)


def _attention(q, k, v, tc, need_ctx, tq=256):
    bsz, _, t, _ = q.shape
    gw = ATT_GROUP * HEAD_DIM
    assert tc % tq == 0 and (t - tc) % (2 * tq) == 0
    first = tc // tq
    n2 = (t - tc) // (2 * tq)
    rows = ATT_GROUP * tq
    half_shape = jax.ShapeDtypeStruct((bsz, ATT_KV_HEADS, n2 * tq, gw), BF16)
    y_even, y_odd = pl.pallas_call(
        functools.partial(_attn_lat_kernel, tq=tq),
        grid=(bsz, ATT_KV_HEADS, n2 + 1),
        in_specs=[
            pl.BlockSpec((1, ATT_GROUP, tq, HEAD_DIM),
                         lambda b, j, i: (b, j, first + 2 * jnp.minimum(i, n2 - 1), 0)),
            pl.BlockSpec((1, ATT_GROUP, tq, HEAD_DIM),
                         lambda b, j, i: (b, j, first + 2 * jnp.minimum(i, n2 - 1) + 1, 0)),
            pl.BlockSpec((1, 1, t, HEAD_DIM), lambda b, j, i: (b, j, 0, 0)),
            pl.BlockSpec((1, 1, t, LANES), lambda b, j, i: (b, j, 0, 0)),
        ],
        out_specs=[pl.BlockSpec((1, 1, tq, gw), lambda b, j, i: (b, j, jnp.minimum(i, n2 - 1), 0)),
                   pl.BlockSpec((1, 1, tq, gw), lambda b, j, i: (b, j, jnp.maximum(i - 1, 0), 0))],
        out_shape=[half_shape, half_shape],
        scratch_shapes=[pltpu.VMEM((rows, t), F32), pltpu.VMEM((rows, t), F32),
                        pltpu.VMEM((rows, LANES), F32), pltpu.VMEM((rows, LANES), F32)],
        compiler_params=_params(3),
        name="attention",
    )(q, q, k, v)
    y_lat = jnp.stack([y_even.reshape(bsz, ATT_KV_HEADS, n2, tq, gw),
                       y_odd.reshape(bsz, ATT_KV_HEADS, n2, tq, gw)], axis=3).reshape(bsz, ATT_KV_HEADS, t - tc, gw)
    if not need_ctx:
        return None, y_lat
    y_ctx = pl.pallas_call(
        functools.partial(_attn_ctx_kernel, tc=tc),
        grid=(bsz, ATT_KV_HEADS),
        in_specs=[
            pl.BlockSpec((1, ATT_GROUP, tc, HEAD_DIM), lambda b, j: (b, j, 0, 0)),
            pl.BlockSpec((1, 1, tc, HEAD_DIM), lambda b, j: (b, j, 0, 0)),
            pl.BlockSpec((1, 1, tc, LANES), lambda b, j: (b, j, 0, 0)),
        ],
        out_specs=pl.BlockSpec((1, 1, tc, gw), lambda b, j: (b, j, 0, 0)),
        out_shape=jax.ShapeDtypeStruct((bsz, ATT_KV_HEADS, tc, gw), BF16),
        compiler_params=_params(2),
        name="attention_ctx",
    )(q, k, v)
    return y_ctx, y_lat


HALO = SUBLANES


def _dn_prep_rows(xe_ref, pg, cw_ref, red_ref, exp_ref, al_ref, dt_ref, qkv_out, g_out, r0):
    n = pg.shape[0]
    rows = slice(r0, r0 + n)
    first = HALO - DN_CONV // 2
    acc = cw_ref[0:1, :] * xe_ref[first:first + n, :]
    for j in range(1, DN_CONV):
        acc = acc + cw_ref[j:j + 1, :] * xe_ref[first + j:first + j + n, :]
    y = _silu(acc)
    qk = y[:, :2 * DN_WIDTH]
    qk = qk * _head_rsqrt(qk, red_ref, exp_ref)
    qkv_out[0, rows, :DN_WIDTH] = qk[:, :DN_WIDTH] * (HEAD_DIM ** -0.5)
    qkv_out[0, rows, DN_WIDTH:2 * DN_WIDTH] = qk[:, DN_WIDTH:]
    qkv_out[0, rows, 2 * DN_WIDTH:] = y[:, 2 * DN_WIDTH:]
    z = pg + dt_ref[...]
    softplus = jnp.maximum(z, 0.0) + jnp.log1p(jnp.exp(-jnp.abs(z)))
    lane = lax.broadcasted_iota(jnp.int32, pg.shape, 1)
    g_out[0, rows, :] = jnp.where(lane < 2 * DN_HEADS, -jnp.exp(al_ref[...]) * softplus, _sigmoid(pg))


def _pre_mixer_kernel(*refs, tm, ncb, nblk, nsrc):
    rows_refs, rest = refs[:3 * nsrc], refs[3 * nsrc:]
    (sh_ref, sc_ref, w_ref, lng_ref, lnb_ref, ws_ref, bias_ref,
     cos_ref, sin_ref, again_ref, ared_ref, aexp_ref, cw_ref, dred_ref, dexp_ref, al_ref, dt_ref,
     ya_out, q_out, k_out, v_out, dqkv_out, g_out, z_out, xe_ref) = rest
    i = pl.program_id(1)
    seg_start = jnp.logical_or(i == 0, i == ncb)
    seg_end = jnp.logical_or(i == ncb - 1, i == nblk - 1)
    cand = [jnp.concatenate([rows_refs[3 * s + 1][0], rows_refs[3 * s][0], rows_refs[3 * s + 2][0]], axis=0)
            for s in range(nsrc)]
    rows = cand[0] if nsrc == 1 else jnp.where(i < ncb, cand[0], cand[1])
    hf = _norm(rows) * (1.0 + sc_ref[...]) + sh_ref[...]
    o_b, o_q, o_z, o_g = A_COLS, A_COLS + B_COLS, A_COLS + B_COLS + QKV_COLS, IN_COLS - GATE_COLS
    n = MLP_CHUNK
    ngroups = tm // n

    def project(g):
        r0 = g * n
        h = hf[HALO + r0:HALO + r0 + n].astype(BF16)
        xe_ref[g] = _dot(hf[r0:r0 + n + 2 * HALO].astype(BF16), w_ref[:, o_q:o_z])
        if g == 0:
            xe_ref[g, 0:HALO, :] = jnp.where(seg_start, 0.0, xe_ref[g, 0:HALO, :])
        if g == ngroups - 1:
            xe_ref[g, HALO + n:2 * HALO + n, :] = jnp.where(seg_end, 0.0, xe_ref[g, HALO + n:2 * HALO + n, :])
        z_out[0, r0:r0 + n, :] = _dot(h, w_ref[:, o_z:o_g]).astype(BF16)
        return _dot(h, w_ref[:, :o_b]), _dot(h, w_ref[:, o_b:o_q]), _dot(h, w_ref[:, o_g:])

    def finish(g, pa, pb, pg):
        r0 = g * n
        _gmlp_rows(pa, lng_ref, lnb_ref, ws_ref, bias_ref, ya_out, r0)
        _attn_prep_rows(pb, cos_ref, sin_ref, again_ref, ared_ref, aexp_ref, q_out, k_out, v_out, r0)
        _dn_prep_rows(xe_ref.at[g], pg, cw_ref, dred_ref, dexp_ref, al_ref, dt_ref, dqkv_out, g_out, r0)

    pending = project(0)
    for g in range(1, ngroups):
        nxt = project(g)
        finish(g - 1, *pending)
        pending = nxt
    finish(ngroups - 1, *pending)


def _row_block_sources(src, tm, ncb, first=0, halo=False):
    parts = src if isinstance(src, tuple) else (src,)
    starts = (0, ncb) if len(parts) == 2 else (0,)
    hb = tm // HALO
    arrays, specs = [], []
    for arr, start in zip(parts, starts):
        d = arr.shape[-1]
        nb = arr.shape[1] // tm

        def blk(i, start=start, nb=nb):
            return jnp.clip(i + first - start, 0, nb - 1)

        arrays.append(arr)
        specs.append(pl.BlockSpec((1, tm, d), lambda b, i, blk=blk: (b, blk(i), 0)))
        if halo:
            arrays += [arr, arr]
            specs.append(pl.BlockSpec((1, HALO, d), lambda b, i, blk=blk: (b, jnp.maximum(blk(i) * hb - 1, 0), 0)))
            specs.append(pl.BlockSpec(
                (1, HALO, d), lambda b, i, blk=blk, nb=nb: (b, jnp.minimum((blk(i) + 1) * hb, nb * hb - 1), 0)))
    return arrays, specs


def _pre_mixer(src, modl, w_in_p, ln_g, ln_b, w_s, b_s, cos_t, sin_t, q_g, k_g, conv_w, a_log, dt_bias,
               tm, ncb, ctx_row):
    parts = src if isinstance(src, tuple) else (src,)
    bsz, _, d = parts[0].shape
    t = sum(p.shape[1] for p in parts)
    nblk = t // tm
    row_arrays, row_specs = _row_block_sources(src, tm, ncb, halo=True)
    nqk = ATT_WIDTH + KV_WIDTH
    bias = jnp.repeat(b_s.T, HEAD_DIM, axis=1)
    gain = jnp.concatenate([jnp.tile(q_g, ATT_Q_HEADS), jnp.tile(k_g, ATT_KV_HEADS)]).reshape(1, nqk)
    ared, aexp = _head_maps(nqk, 1.0 / HEAD_DIM)
    dred, dexp = _head_maps(2 * DN_WIDTH, 1.0)
    pad = lambda v: jnp.zeros((1, LANES), F32).at[0, :2 * DN_HEADS].set(v.reshape(-1))
    mspec = lambda c: pl.BlockSpec((None, None, 1, d), lambda b, i: (jnp.where(i < ncb, ctx_row, b), c, 0, 0))
    rspec = lambda w: pl.BlockSpec((1, tm, w), lambda b, i: (b, i, 0))
    hspec = lambda nh, w: pl.BlockSpec((1, nh, tm, w), lambda b, i: (b, 0, i, 0))
    heads = ((ATT_Q_HEADS, HEAD_DIM), (ATT_KV_HEADS, HEAD_DIM), (ATT_KV_HEADS, LANES))
    rows_out = ((QKV_COLS, F32), (LANES, F32), (DN_WIDTH, BF16))
    return pl.pallas_call(
        functools.partial(_pre_mixer_kernel, tm=tm, ncb=ncb, nblk=nblk, nsrc=len(parts)),
        grid=(bsz, nblk),
        in_specs=row_specs + [
            mspec(0), mspec(1),
            pl.BlockSpec(w_in_p.shape, lambda b, i: (0, 0), pipeline_mode=pl.Buffered(1)),
            _const_spec((1, A_WIDTH)), _const_spec((1, A_WIDTH)), _const_spec(w_s.shape), _const_spec(bias.shape),
            pl.BlockSpec((tm, nqk), lambda b, i: (i, 0)), pl.BlockSpec((tm, nqk), lambda b, i: (i, 0)),
            _const_spec((1, nqk)), _const_spec(ared.shape), _const_spec(aexp.shape),
            _const_spec(conv_w.shape), _const_spec(dred.shape), _const_spec(dexp.shape),
            _const_spec((1, LANES)), _const_spec((1, LANES)),
        ],
        out_specs=[rspec(A_WIDTH)] + [hspec(nh, w) for nh, w in heads] + [rspec(w) for w, _ in rows_out],
        out_shape=[jax.ShapeDtypeStruct((bsz, t, A_WIDTH), BF16)]
        + [jax.ShapeDtypeStruct((bsz, nh, t, w), BF16) for nh, w in heads]
        + [jax.ShapeDtypeStruct((bsz, t, w), dt) for w, dt in rows_out],
        scratch_shapes=[pltpu.VMEM((tm // MLP_CHUNK, MLP_CHUNK + 2 * HALO, QKV_COLS), F32)],
        compiler_params=_params(2),
        name="pre_mixer",
    )(*row_arrays, modl, modl, w_in_p,
      ln_g.reshape(1, -1), ln_b.reshape(1, -1), w_s.astype(BF16), bias,
      cos_t, sin_t, gain, ared, aexp, conv_w, dred, dexp, pad(a_log), pad(dt_bias))


TILE_A = 4 * HEAD_DIM


def _split3(x):
    hi = x.astype(BF16)
    r = x - hi.astype(F32)
    mid = r.astype(BF16)
    lo = (r - mid.astype(F32)).astype(BF16)
    return hi, mid, lo


def _head_id(shape, axis):
    return jnp.right_shift(lax.broadcasted_iota(jnp.int32, shape, axis), 6)


class _Packed:
    def __init__(self):
        c, w = DN_CHUNK, DN_WIDTH
        wb = w - TILE_A
        self.mask_a = _head_id((TILE_A, TILE_A), 0) == _head_id((TILE_A, TILE_A), 1)
        self.mask_b = _head_id((wb, wb), 0) == _head_id((wb, wb), 1)
        self.ri = lax.broadcasted_iota(jnp.int32, (c, w), 0)
        self.cj = lax.broadcasted_iota(jnp.int32, (c, w), 1) & (HEAD_DIM - 1)
        self.eye = (self.ri == self.cj).astype(F32)
        self.incl = (self.cj <= self.ri, self.cj >= self.ri)
        self.strict = (self.cj < self.ri, self.cj > self.ri)
        r64 = lax.broadcasted_iota(jnp.int32, (c, c), 0)
        c64 = lax.broadcasted_iota(jnp.int32, (c, c), 1)
        self.order = ((c64 <= r64).astype(BF16), (c64 >= r64).astype(BF16))
        self.blk = {b: jnp.right_shift(self.ri, int(math.log2(b))) == jnp.right_shift(self.cj, int(math.log2(b)))
                    for b in (8, 16, 32)}

    def weights(self, y):
        y16 = y.astype(BF16)
        ya = jnp.concatenate([y16[:, :TILE_A]] * 4, axis=0)
        yb = jnp.concatenate([y16[:, TILE_A:]] * 2, axis=0)
        zero = jnp.zeros((), BF16)
        return jnp.where(self.mask_a, ya, zero), jnp.where(self.mask_b, yb, zero)

    def dot(self, x, y):
        wa, wb = self.weights(y)
        x16 = x.astype(BF16)
        return jnp.concatenate([_dot(x16[:, :TILE_A], wa), _dot(x16[:, TILE_A:], wb)], axis=-1)

    def dot_nt(self, x, y):
        wa, wb = self.weights(y)
        x16 = x.astype(BF16)
        return jnp.concatenate([_dot_nt(x16[:, :TILE_A], wa), _dot_nt(x16[:, TILE_A:], wb)], axis=-1)

    def dot_tn(self, x, y):
        x16, y16 = x.astype(BF16), y.astype(BF16)
        ta = _dot_tn(x16[:, :TILE_A], y16[:, :TILE_A])
        tb = _dot_tn(x16[:, TILE_A:], y16[:, TILE_A:])
        return jnp.where(self.mask_a, ta, 0.0), jnp.where(self.mask_b, tb, 0.0)

    def tri_inverse(self, lows):
        c = DN_CHUNK
        nb = [-jnp.where(self.blk[8], low, 0.0) for low in lows]
        x = [self.eye + a for a in nb]
        pw = [self.dot(a, a) for a in nb]
        st = [self.dot(jnp.concatenate([xa, p], axis=0), p) for xa, p in zip(x, pw)]
        x = [xa + s[:c] for xa, s in zip(x, st)]
        x = [xa + self.dot(xa, s[c:]) for xa, s in zip(x, st)]
        for b in (8, 16, 32):
            inner = self.blk[b]
            outer = self.blk[2 * b] if 2 * b in self.blk else None
            off = jnp.logical_not(inner) if outer is None else jnp.logical_and(outer, jnp.logical_not(inner))
            xc = [self.dot(xa, jnp.where(off, low, 0.0)) for xa, low in zip(x, lows)]
            x = [xa - self.dot(t, xa) for xa, t in zip(x, xc)]
        return x


def _spread_heads(x, first):
    c = x.shape[0]
    low_half = lax.broadcasted_iota(jnp.int32, (c, LANES), 1) < HEAD_DIM
    tiles = []
    for j in range(0, DN_HEADS, 2):
        a = jnp.broadcast_to(x[:, first + j:first + j + 1], (c, LANES))
        b = jnp.broadcast_to(x[:, first + j + 1:first + j + 2], (c, LANES))
        tiles.append(jnp.where(low_half, a, b))
    return jnp.concatenate(tiles, axis=-1)


def _dn_chunk_terms(pk, dirs, q, k, v, gates):
    c = DN_CHUNK
    incl = [pk.incl[d] for d in dirs]
    strict = [pk.strict[d] for d in dirs]
    gc = [sum(_dot(pk.order[d], p) for p in _split3(g)) for d, g in zip(dirs, gates)]
    gcx = [_spread_heads(a, d * DN_HEADS) for d, a in zip(dirs, gc)]
    bx = [_spread_heads(g, (2 + d) * DN_HEADS) for d, g in zip(dirs, gates)]
    gtot = [a[c - 1:c, :] if d == 0 else a[0:1, :] for d, a in zip(dirs, gcx)]
    grow = [jnp.sum(pk.eye * a, 0, keepdims=True) for a in gcx]
    decay = [jnp.where(m, jnp.exp(jnp.minimum(a - r, 0.0)), 0.0) for m, a, r in zip(incl, gcx, grow)]
    kbeta = [a * b for a, b in zip(k, bx)]
    kq = [pk.dot_nt(jnp.concatenate([a, b], axis=0), kk) for a, b, kk in zip(kbeta, q, k)]
    low = [jnp.where(m, a[:c] * dc, 0.0) for m, a, dc in zip(strict, kq, decay)]
    intra = [jnp.where(m, a[c:] * dc, 0.0) for m, a, dc in zip(incl, kq, decay)]
    tinv = pk.tri_inverse(low)
    eg = [jnp.exp(a) for a in gcx]
    u = [pk.dot(t, a * b) for t, a, b in zip(tinv, v, bx)]
    wm = [pk.dot(t, a * e) for t, a, e in zip(tinv, kbeta, eg)]
    lhs = [jnp.concatenate([a * e, w_], axis=0).astype(BF16) for a, e, w_ in zip(q, eg, wm)]
    k_dec = [a * jnp.exp(gt - gx) for a, gt, gx in zip(k, gtot, gcx)]
    egl = [jnp.exp(gt) for gt in gtot]
    return [dict(lhs=a, u=b, intra=i, k_dec=kd, egl=e) for a, b, i, kd, e in zip(lhs, u, intra, k_dec, egl)]


def _dn_scan_kernel(xf, gf, xb, gb, of, ob, sa_ref, sb_ref, *, nsub):
    c, w = DN_CHUNK, DN_WIDTH

    @pl.when(pl.program_id(1) == 0)
    def _():
        sa_ref[...] = jnp.zeros_like(sa_ref)
        sb_ref[...] = jnp.zeros_like(sb_ref)

    pk = _Packed()
    refs = ((xf, gf, of), (xb, gb, ob))
    items = [(d, j if d == 0 else nsub - 1 - j) for j in range(nsub) for d in (0, 1)]
    rows = [slice(j * c, (j + 1) * c) for _, j in items]
    dirs = [d for d, _ in items]
    terms = _dn_chunk_terms(
        pk, dirs,
        [refs[d][0][0, r, 0:w] for d, r in zip(dirs, rows)],
        [refs[d][0][0, r, w:2 * w] for d, r in zip(dirs, rows)],
        [refs[d][0][0, r, 2 * w:3 * w] for d, r in zip(dirs, rows)],
        [refs[d][1][0, r, :] for d, r in zip(dirs, rows)])
    state = [(sa_ref[d], sb_ref[d]) for d in (0, 1)]
    for d, r, t in zip(dirs, rows, terms):
        sa, sb = state[d]
        lhs = t["lhs"]
        res = jnp.concatenate([_dot(lhs[:, :TILE_A], sa.astype(BF16)), _dot(lhs[:, TILE_A:], sb.astype(BF16))],
                              axis=-1)
        v_new = t["u"] - res[c:]
        refs[d][2][0, r, :] = res[:c] + pk.dot(t["intra"], v_new)
        ta, tb = pk.dot_tn(t["k_dec"], v_new)
        state[d] = (sa * t["egl"][:, :TILE_A] + ta, sb * t["egl"][:, TILE_A:] + tb)
    for d in (0, 1):
        sa_ref[d], sb_ref[d] = state[d]


DN_BLOCKS = (4 * DN_CHUNK, 2 * DN_CHUNK)


def _dn_scan(qkv, g, tc):
    bsz, t, w3 = qkv.shape
    w = w3 // 3
    c = next(blk for blk in DN_BLOCKS if tc % blk == 0 and t % blk == 0)
    nct, ncx = t // c, tc // c

    def bwd_blk(s):
        return jnp.where(s < ncx, ncx - 1 - s, nct - 1 - s + ncx)

    fspec = lambda width: pl.BlockSpec((1, c, width), lambda b, s: (b, s, 0))
    bspec = lambda width: pl.BlockSpec((1, c, width), lambda b, s: (b, bwd_blk(s), 0))
    return pl.pallas_call(
        functools.partial(_dn_scan_kernel, nsub=c // DN_CHUNK),
        grid=(bsz, nct),
        in_specs=[fspec(w3), fspec(LANES), bspec(w3), bspec(LANES)],
        out_specs=[fspec(w), bspec(w)],
        out_shape=[jax.ShapeDtypeStruct((bsz, t, w), F32)] * 2,
        scratch_shapes=[pltpu.VMEM((2, TILE_A, TILE_A), F32), pltpu.VMEM((2, w - TILE_A, w - TILE_A), F32)],
        compiler_params=_params(2),
        name="dn_scan",
    )(qkv, g, qkv, g)


def _post_kernel(*refs, alpha, f_chunk, nsplit, nsrc, ncb):
    x_refs, rest = refs[:nsrc], refs[nsrc:]
    (ya_ref, yb0_ref, yb1_ref, of_ref, ob_ref, z_ref, ngain_ref, nred_ref, nexp_ref,
     g1_ref, sh2_ref, sc2_ref, g2_ref, wa_ref, wb0_ref, wb1_ref, wc_ref, l1g_ref, l1b_ref, l2g_ref, l2b_ref,
     wup_ref, wdn_ref, o_ref) = rest
    tm = o_ref.shape[1]
    groups = [slice(r * tm // nsplit, (r + 1) * tm // nsplit) for r in range(nsplit)]
    is_ctx = pl.program_id(1) < ncb

    def x_rows(r):
        return x_refs[0][0, r, :] if nsrc == 1 else jnp.where(is_ctx, x_refs[0][0, r, :], x_refs[1][0, r, :])

    o = [of_ref[0, r, :] + ob_ref[0, r, :] for r in groups]
    yc = [(a * _head_rsqrt(a, nred_ref, nexp_ref) * ngain_ref[...]
           * _silu(z_ref[0, r, :].astype(F32))).astype(BF16) for a, r in zip(o, groups)]
    branch = [(_dot(ya_ref[0, r, :], wa_ref[...]) + _dot(yb0_ref[0, 0, r, :], wb0_ref[...])
               + _dot(yb1_ref[0, 0, r, :], wb1_ref[...]) + _dot(c, wc_ref[...])) for c, r in zip(yc, groups)]
    x1 = [_norm(alpha * x_rows(r) + g1_ref[...] * br) * l1g_ref[...] + l1b_ref[...]
          for br, r in zip(branch, groups)]
    h = [(_norm(a) * (1.0 + sc2_ref[...]) + sh2_ref[...]).astype(BF16) for a in x1]
    d_ff = wup_ref.shape[1]
    m = [None] * nsplit
    for f in range(0, d_ff, f_chunk):
        up = [jnp.maximum(_dot(a, wup_ref[:, f:f + f_chunk]), 0.0) for a in h]
        part = [_dot((a * a).astype(BF16), wdn_ref[f:f + f_chunk, :]) for a in up]
        m = [p if acc is None else acc + p for acc, p in zip(m, part)]
    for r, a, mm in zip(groups, x1, m):
        o_ref[0, r, :] = _norm(alpha * a + g2_ref[...] * mm) * l2g_ref[...] + l2b_ref[...]


def _post(src, ya, yb, o_f, o_b, pz, norm_g, modl, w_out, ln1_g, ln1_b, ln2_g, ln2_b, w_up, w_down,
          tm, ncb, ctx_row, alpha, need_ctx):
    parts = src if isinstance(src, tuple) else (src,)
    bsz, _, d = parts[0].shape
    t = sum(p.shape[1] for p in parts)
    assert need_ctx or len(parts) == 1
    ngain = jnp.tile(norm_g, DN_HEADS).reshape(1, DN_WIDTH)
    nred, nexp = _head_maps(DN_WIDTH, 1.0 / HEAD_DIM)
    first = 0 if need_ctx else ncb
    yb_first = first - (t - yb.shape[2]) // tm
    gw = ATT_GROUP * HEAD_DIM
    o1 = A_WIDTH
    wa, wb0, wb1, wc = w_out[:o1], w_out[o1:o1 + gw], w_out[o1 + gw:o1 + 2 * gw], w_out[o1 + 2 * gw:]
    mspec = lambda c: pl.BlockSpec(
        (None, None, 1, d), lambda b, i: (jnp.where(i + first < ncb, ctx_row, b), c, 0, 0))
    rspec = lambda w: pl.BlockSpec((1, tm, w), lambda b, i: (b, i + first, 0))
    vec = lambda a: a.reshape(1, d)
    wspec = lambda a: pl.BlockSpec(a.shape, lambda b, i: (0, 0), pipeline_mode=pl.Buffered(1))
    weights = [w.astype(BF16) for w in (wa, wb0, wb1, wc)]
    w_up, w_down = w_up.astype(BF16), w_down.astype(BF16)
    x_arrays, x_specs = _row_block_sources(src, tm, ncb, first=first)
    return pl.pallas_call(
        functools.partial(_post_kernel, alpha=alpha, f_chunk=1024, nsplit=2, nsrc=len(parts), ncb=ncb),
        grid=(bsz, t // tm - first),
        in_specs=x_specs + [
            rspec(A_WIDTH),
            pl.BlockSpec((1, 1, tm, gw), lambda b, i: (b, 0, i + yb_first, 0)),
            pl.BlockSpec((1, 1, tm, gw), lambda b, i: (b, 1, i + yb_first, 0)),
            rspec(DN_WIDTH), rspec(DN_WIDTH), rspec(DN_WIDTH),
            _const_spec(ngain.shape), _const_spec(nred.shape), _const_spec(nexp.shape),
            mspec(2), mspec(3), mspec(4), mspec(5),
            *[wspec(w) for w in weights],
            _const_spec((1, d)), _const_spec((1, d)), _const_spec((1, d)), _const_spec((1, d)),
            wspec(w_up), wspec(w_down),
        ],
        out_specs=pl.BlockSpec((1, tm, d), lambda b, i: (b, i, 0)),
        out_shape=jax.ShapeDtypeStruct((bsz, t - first * tm, d), F32),
        compiler_params=_params(2),
        name="post_mixer",
    )(*x_arrays, ya, yb, yb, o_f, o_b, pz, ngain, nred, nexp, modl, modl, modl, modl, *weights,
      vec(ln1_g), vec(ln1_b), vec(ln2_g), vec(ln2_b), w_up, w_down)


def _rope_tables(tc, tl):
    pos = jnp.arange(tl)
    row = (pos // GRID_W).astype(F32)
    col = (pos % GRID_W).astype(F32)
    half = HEAD_DIM // 2
    inv = 1.0 / (ROPE_THETA ** (jnp.arange(0, half, 2, dtype=F32) / half))
    ar, ac = row[:, None] * inv, col[:, None] * inv
    cos = jnp.concatenate([jnp.cos(ar), jnp.cos(ar), jnp.cos(ac), jnp.cos(ac)], -1)
    sin = jnp.concatenate([-jnp.sin(ar), jnp.sin(ar), -jnp.sin(ac), jnp.sin(ac)], -1)
    cos = jnp.concatenate([jnp.ones((tc, HEAD_DIM), F32), cos], 0)
    sin = jnp.concatenate([jnp.zeros((tc, HEAD_DIM), F32), sin], 0)
    scale = HEAD_DIM ** -0.5 * math.log2(math.e)
    tile =lambda a: jnp.concatenate([jnp.tile(a, (1, ATT_Q_HEADS)) * scale, jnp.tile(a, (1, ATT_KV_HEADS))], -1)
    return tile(cos), tile(sin)


def kernel(x, c, ctx, c_ctx, mod_w, mod_b, w_in, w_out, gmlp_ln_g, gmlp_ln_b, gmlp_w_s, gmlp_b_s,
           attn_q_g, attn_k_g, dn_conv_w, dn_a_log, dn_dt_bias, dn_norm_g,
           ln1_g, ln1_b, ln2_g, ln2_b, w_up, w_down):
    bsz, tl, d = x.shape
    tc = ctx.shape[1]
    depth = mod_w.shape[0]
    assert bsz < MOD_ROWS and tl % GRID_W == 0 and tc % MLP_CHUNK == 0 and tl % MLP_CHUNK == 0
    tm = 256 if tc % 256 == 0 else MLP_CHUNK
    ncb = tc // tm
    ctx_row = bsz
    alpha = (2 * depth) ** 0.25

    cs = jnp.zeros((MOD_ROWS, d), F32).at[:bsz].set(c).at[ctx_row].set(c_ctx)
    mod = _modulation(cs, mod_w, mod_b).reshape(depth, MOD_ROWS, 6, 1, d)
    cos_t, sin_t = _rope_tables(tc, tl)
    xcat = (ctx, x) if depth > 1 else jnp.concatenate([ctx, x], axis=1)

    for l in range(depth):
        need_ctx = l < depth - 1
        w_in_p = jnp.pad(w_in[l], ((0, 0), (0, IN_COLS_PAD - IN_COLS))).astype(BF16)
        ya, qa, ka, va, dqkv, dg, pz = _pre_mixer(
            xcat, mod[l], w_in_p, gmlp_ln_g[l], gmlp_ln_b[l], gmlp_w_s[l], gmlp_b_s[l], cos_t, sin_t,
            attn_q_g[l], attn_k_g[l], dn_conv_w[l], dn_a_log[l], dn_dt_bias[l], tm, ncb, ctx_row)
        yb_ctx, yb = _attention(qa, ka, va, tc, need_ctx)
        if need_ctx:
            yb = jnp.concatenate([yb_ctx, yb], axis=2)
        o_f, o_b = _dn_scan(dqkv, dg, tc)
        xcat = _post(xcat, ya, yb, o_f, o_b, pz, dn_norm_g[l], mod[l], w_out[l], ln1_g[l], ln1_b[l], ln2_g[l], ln2_b[l],
                     w_up[l], w_down[l], tm, ncb, ctx_row, alpha, need_ctx)
    return xcat
```

```python
import functools
import math

import jax
import jax.numpy as jnp
from jax import lax
from jax.experimental import pallas as pl
from jax.experimental.pallas import tpu as pltpu

F32 = jnp.float32
BF16 = jnp.bfloat16

GRID_W = 64
HEAD_DIM = 64
A_GROUPS = 4
A_WIDTH = A_GROUPS * HEAD_DIM
MLP_CHUNK = 128
ATT_Q_HEADS = 6
ATT_KV_HEADS = 2
ATT_GROUP = ATT_Q_HEADS // ATT_KV_HEADS
ATT_WIDTH = ATT_Q_HEADS * HEAD_DIM
KV_WIDTH = ATT_KV_HEADS * HEAD_DIM
ROPE_THETA = 10000.0
DN_HEADS = 6
DN_WIDTH = DN_HEADS * HEAD_DIM
DN_CONV = 5
DN_CHUNK = 64
A_COLS = 2 * A_WIDTH
B_COLS = ATT_WIDTH + 2 * KV_WIDTH
QKV_COLS = 3 * DN_WIDTH
GATE_COLS = 4 * DN_HEADS
IN_COLS = A_COLS + B_COLS + QKV_COLS + DN_WIDTH + GATE_COLS
LANES = 128
SUBLANES = 8
IN_COLS_PAD = IN_COLS - GATE_COLS + LANES
EPS = 1e-6
MOD_ROWS = 8
VMEM_LIMIT = 56 * 1024 * 1024


def _dot(a, b):
    return jnp.dot(a, b, preferred_element_type=F32)


def _dot_nt(a, b):
    return lax.dot_general(a, b, (((1,), (1,)), ((), ())), preferred_element_type=F32)


def _dot_tn(a, b):
    return lax.dot_general(a, b, (((0,), (0,)), ((), ())), preferred_element_type=F32)


def _split(x):
    hi = x.astype(BF16)
    lo = (x - hi.astype(F32)).astype(BF16)
    return hi, lo


def _dot_x3(a, b):
    ah, al = _split(a)
    bh, bl = _split(b)
    return _dot(ah, bh) + (_dot(ah, bl) + _dot(al, bh))


def _dot_lhs_split(x, m):
    hi, lo = _split(x)
    return _dot(hi, m) + _dot(lo, m)


def _norm(x):
    mu = jnp.mean(x, -1, keepdims=True)
    xc = x - mu
    var = jnp.mean(xc * xc, -1, keepdims=True)
    return xc * lax.rsqrt(var + EPS)


def _sigmoid(x):
    return 1.0 / (1.0 + jnp.exp(-x))


def _silu(x):
    return x * _sigmoid(x)


def _params(n_grid, vmem=VMEM_LIMIT):
    return pltpu.CompilerParams(dimension_semantics=("arbitrary",) * n_grid, vmem_limit_bytes=vmem)


def _const_spec(shape):
    nd = len(shape)
    return pl.BlockSpec(shape, lambda *_: (0,) * nd)


def _mod_kernel(cs_ref, w_ref, b_ref, o_ref):
    cs = cs_ref[...]
    o_ref[0] = _dot_x3(_silu(cs), w_ref[0]) + b_ref[0]


def _modulation(cs, mod_w, mod_b):
    depth, d, n = mod_w.shape
    tn = 1536
    return pl.pallas_call(
        _mod_kernel,
        grid=(depth, n // tn),
        in_specs=[
            pl.BlockSpec((MOD_ROWS, d), lambda l, j: (0, 0)),
            pl.BlockSpec((1, d, tn), lambda l, j: (l, 0, j)),
            pl.BlockSpec((1, 1, tn), lambda l, j: (l, 0, j)),
        ],
        out_specs=pl.BlockSpec((1, MOD_ROWS, tn), lambda l, j: (l, 0, j)),
        out_shape=jax.ShapeDtypeStruct((depth, MOD_ROWS, n), F32),
        compiler_params=_params(2),
        name="modulation",
    )(cs, mod_w, mod_b.reshape(depth, 1, n))


def _gmlp_rows(pa, lng_ref, lnb_ref, ws_ref, bias_ref, o_ref, r0):
    a = 0.5 * pa * (1.0 + lax.erf(pa * (2.0 ** -0.5)))
    u = a[:, :A_WIDTH]
    v = _norm(a[:, A_WIDTH:]) * lng_ref[...] + lnb_ref[...]
    vb = v.astype(BF16)
    mixed = jnp.concatenate(
        [_dot(ws_ref[g], vb[:, g * HEAD_DIM:(g + 1) * HEAD_DIM]) for g in range(A_GROUPS)], axis=-1)
    o_ref[0, r0:r0 + MLP_CHUNK, :] = (u * (mixed + bias_ref[...])).astype(BF16)


def _swap_rope_pairs(x):
    n = x.shape[-1]
    lane = lax.broadcasted_iota(jnp.int32, x.shape, x.ndim - 1)
    first = (lane & 31) < 16
    return jnp.where(first, pltpu.roll(x, n - 16, x.ndim - 1), pltpu.roll(x, 16, x.ndim - 1))


def _head_rsqrt(x, red_ref, exp_ref):
    s = _dot_lhs_split(x * x, red_ref[...])
    return _dot_lhs_split(lax.rsqrt(s + EPS), exp_ref[...])


def _head_maps(width, value):
    head = jnp.arange(width) // HEAD_DIM
    col = jnp.arange(LANES)
    red = jnp.where(head[:, None] == col[None, :], value, 0.0).astype(BF16)
    return red, (col[:, None] == head[None, :]).astype(BF16)


def _attn_prep_rows(pb, cos_ref, sin_ref, gain_ref, red_ref, exp_ref, q_out, k_out, v_out, r0):
    nqk = ATT_WIDTH + KV_WIDTH
    rows = slice(r0, r0 + pb.shape[0])
    qk = pb[:, :nqk]
    qk = qk * _head_rsqrt(qk, red_ref, exp_ref) * gain_ref[...]

    def per_head(tab_ref):
        tab = tab_ref[rows, :]
        swapped = pltpu.roll(tab, HEAD_DIM, 1)
        low = lax.broadcasted_iota(jnp.int32, tab.shape, 1) < HEAD_DIM
        q2, k2 = jnp.where(low, tab, swapped), jnp.where(low, swapped, tab)
        return jnp.concatenate([q2] * (ATT_Q_HEADS // 2) + [k2] * (ATT_KV_HEADS // 2), axis=-1)

    r = qk * per_head(cos_ref) + _swap_rope_pairs(qk) * per_head(sin_ref)
    for h in range(ATT_Q_HEADS):
        q_out[0, h, rows, :] = r[:, h * HEAD_DIM:(h + 1) * HEAD_DIM].astype(BF16)
    vv = pb[:, nqk:nqk + KV_WIDTH]
    lane = lax.broadcasted_iota(jnp.int32, vv.shape, 1)
    for h in range(ATT_KV_HEADS):
        k_out[0, h, rows, :] = r[:, ATT_WIDTH + h * HEAD_DIM:ATT_WIDTH + (h + 1) * HEAD_DIM].astype(BF16)
        vh = vv if h == 0 else pltpu.roll(vv, KV_WIDTH - h * HEAD_DIM, 1)
        v_out[0, h, rows, :] = jnp.where(lane < HEAD_DIM, vh, jnp.where(lane == HEAD_DIM, 1.0, 0.0)).astype(BF16)


KV_BLOCK = 256


def _key_blocks(nkeys):
    return [(s0, min(KV_BLOCK, nkeys - s0)) for s0 in range(0, nkeys, KV_BLOCK)]


def _store_heads(o_ref, acc, tq):
    o = acc[:, :HEAD_DIM] / acc[:, HEAD_DIM:HEAD_DIM + 1]
    for g in range(ATT_GROUP):
        o_ref[0, 0, :, g * HEAD_DIM:(g + 1) * HEAD_DIM] = o[g * tq:(g + 1) * tq].astype(BF16)


def _attn_lat_kernel(qe_ref, qo_ref, k_ref, v_ref, oe_ref, oo_ref, sa_ref, sb_ref, ma_ref, mb_ref, *, tq):
    t = k_ref.shape[2]
    i = pl.program_id(2)
    last = pl.num_programs(2) - 1

    def half(q_ref, s_new, m_new, s_old, m_old, o_ref, scores=True, finish=True):
        if scores:
            q = q_ref[0].reshape(ATT_GROUP * tq, HEAD_DIM)
        if finish:
            m = jnp.max(m_old[...], -1, keepdims=True)
        mx = None
        acc = None
        for s0, n in _key_blocks(t):
            if scores:
                s = _dot_nt(q, k_ref[0, 0, s0:s0 + n, :])
                s_new[:, s0:s0 + n] = s
                for l0 in range(0, n, LANES):
                    part = s[:, l0:l0 + LANES]
                    mx = part if mx is None else jnp.maximum(mx, part)
            if finish:
                p = jnp.exp2(s_old[:, s0:s0 + n] - m).astype(BF16)
                part = _dot(p, v_ref[0, 0, s0:s0 + n, :])
                acc = part if acc is None else acc + part
        if scores:
            m_new[...] = mx
        if finish:
            _store_heads(o_ref, acc, tq)

    first_half = functools.partial(half, qe_ref, sa_ref, ma_ref, sb_ref, mb_ref, oo_ref)
    pl.when(i == 0)(functools.partial(first_half, finish=False))
    pl.when(jnp.logical_and(i > 0, i < last))(first_half)
    pl.when(i == last)(functools.partial(first_half, scores=False))
    pl.when(i < last)(functools.partial(half, qo_ref, sb_ref, mb_ref, sa_ref, ma_ref, oe_ref))


def _attn_ctx_kernel(q_ref, k_ref, v_ref, o_ref, *, tc):
    q = q_ref[0].reshape(ATT_GROUP * tc, HEAD_DIM)
    s = _dot_nt(q, k_ref[0, 0])
    p = jnp.exp2(s - jnp.max(s, -1, keepdims=True)).astype(BF16)
    _store_heads(o_ref, _dot(p, v_ref[0, 0]), tc)


def _attention(q, k, v, tc, need_ctx, tq=256):
    bsz, _, t, _ = q.shape
    gw = ATT_GROUP * HEAD_DIM
    assert tc % tq == 0 and (t - tc) % (2 * tq) == 0
    first = tc // tq
    n2 = (t - tc) // (2 * tq)
    rows = ATT_GROUP * tq
    half_shape = jax.ShapeDtypeStruct((bsz, ATT_KV_HEADS, n2 * tq, gw), BF16)
    y_even, y_odd = pl.pallas_call(
        functools.partial(_attn_lat_kernel, tq=tq),
        grid=(bsz, ATT_KV_HEADS, n2 + 1),
        in_specs=[
            pl.BlockSpec((1, ATT_GROUP, tq, HEAD_DIM),
                         lambda b, j, i: (b, j, first + 2 * jnp.minimum(i, n2 - 1), 0)),
            pl.BlockSpec((1, ATT_GROUP, tq, HEAD_DIM),
                         lambda b, j, i: (b, j, first + 2 * jnp.minimum(i, n2 - 1) + 1, 0)),
            pl.BlockSpec((1, 1, t, HEAD_DIM), lambda b, j, i: (b, j, 0, 0)),
            pl.BlockSpec((1, 1, t, LANES), lambda b, j, i: (b, j, 0, 0)),
        ],
        out_specs=[pl.BlockSpec((1, 1, tq, gw), lambda b, j, i: (b, j, jnp.minimum(i, n2 - 1), 0)),
                   pl.BlockSpec((1, 1, tq, gw), lambda b, j, i: (b, j, jnp.maximum(i - 1, 0), 0))],
        out_shape=[half_shape, half_shape],
        scratch_shapes=[pltpu.VMEM((rows, t), F32), pltpu.VMEM((rows, t), F32),
                        pltpu.VMEM((rows, LANES), F32), pltpu.VMEM((rows, LANES), F32)],
        compiler_params=_params(3),
        name="attention",
    )(q, q, k, v)
    y_lat = jnp.stack([y_even.reshape(bsz, ATT_KV_HEADS, n2, tq, gw),
                       y_odd.reshape(bsz, ATT_KV_HEADS, n2, tq, gw)], axis=3).reshape(bsz, ATT_KV_HEADS, t - tc, gw)
    if not need_ctx:
        return None, y_lat
    y_ctx = pl.pallas_call(
        functools.partial(_attn_ctx_kernel, tc=tc),
        grid=(bsz, ATT_KV_HEADS),
        in_specs=[
            pl.BlockSpec((1, ATT_GROUP, tc, HEAD_DIM), lambda b, j: (b, j, 0, 0)),
            pl.BlockSpec((1, 1, tc, HEAD_DIM), lambda b, j: (b, j, 0, 0)),
            pl.BlockSpec((1, 1, tc, LANES), lambda b, j: (b, j, 0, 0)),
        ],
        out_specs=pl.BlockSpec((1, 1, tc, gw), lambda b, j: (b, j, 0, 0)),
        out_shape=jax.ShapeDtypeStruct((bsz, ATT_KV_HEADS, tc, gw), BF16),
        compiler_params=_params(2),
        name="attention_ctx",
    )(q, k, v)
    return y_ctx, y_lat


HALO = SUBLANES


def _dn_prep_rows(xe_ref, pg, cw_ref, red_ref, exp_ref, al_ref, dt_ref, qkv_out, g_out, r0):
    n = pg.shape[0]
    rows = slice(r0, r0 + n)
    first = HALO - DN_CONV // 2
    acc = cw_ref[0:1, :] * xe_ref[first:first + n, :]
    for j in range(1, DN_CONV):
        acc = acc + cw_ref[j:j + 1, :] * xe_ref[first + j:first + j + n, :]
    y = _silu(acc)
    qk = y[:, :2 * DN_WIDTH]
    qk = qk * _head_rsqrt(qk, red_ref, exp_ref)
    qkv_out[0, rows, :DN_WIDTH] = qk[:, :DN_WIDTH] * (HEAD_DIM ** -0.5)
    qkv_out[0, rows, DN_WIDTH:2 * DN_WIDTH] = qk[:, DN_WIDTH:]
    qkv_out[0, rows, 2 * DN_WIDTH:] = y[:, 2 * DN_WIDTH:]
    z = pg + dt_ref[...]
    softplus = jnp.maximum(z, 0.0) + jnp.log1p(jnp.exp(-jnp.abs(z)))
    lane = lax.broadcasted_iota(jnp.int32, pg.shape, 1)
    g_out[0, rows, :] = jnp.where(lane < 2 * DN_HEADS, -jnp.exp(al_ref[...]) * softplus, _sigmoid(pg))


def _pre_mixer_kernel(*refs, tm, ncb, nblk, nsrc):
    rows_refs, rest = refs[:3 * nsrc], refs[3 * nsrc:]
    (sh_ref, sc_ref, w_ref, lng_ref, lnb_ref, ws_ref, bias_ref,
     cos_ref, sin_ref, again_ref, ared_ref, aexp_ref, cw_ref, dred_ref, dexp_ref, al_ref, dt_ref,
     ya_out, q_out, k_out, v_out, dqkv_out, g_out, z_out, xe_ref) = rest
    i = pl.program_id(1)
    seg_start = jnp.logical_or(i == 0, i == ncb)
    seg_end = jnp.logical_or(i == ncb - 1, i == nblk - 1)
    cand = [jnp.concatenate([rows_refs[3 * s + 1][0], rows_refs[3 * s][0], rows_refs[3 * s + 2][0]], axis=0)
            for s in range(nsrc)]
    rows = cand[0] if nsrc == 1 else jnp.where(i < ncb, cand[0], cand[1])
    hf = _norm(rows) * (1.0 + sc_ref[...]) + sh_ref[...]
    o_b, o_q, o_z, o_g = A_COLS, A_COLS + B_COLS, A_COLS + B_COLS + QKV_COLS, IN_COLS - GATE_COLS
    n = MLP_CHUNK
    ngroups = tm // n

    def project(g):
        r0 = g * n
        h = hf[HALO + r0:HALO + r0 + n].astype(BF16)
        xe_ref[g] = _dot(hf[r0:r0 + n + 2 * HALO].astype(BF16), w_ref[:, o_q:o_z])
        if g == 0:
            xe_ref[g, 0:HALO, :] = jnp.where(seg_start, 0.0, xe_ref[g, 0:HALO, :])
        if g == ngroups - 1:
            xe_ref[g, HALO + n:2 * HALO + n, :] = jnp.where(seg_end, 0.0, xe_ref[g, HALO + n:2 * HALO + n, :])
        z_out[0, r0:r0 + n, :] = _dot(h, w_ref[:, o_z:o_g]).astype(BF16)
        return _dot(h, w_ref[:, :o_b]), _dot(h, w_ref[:, o_b:o_q]), _dot(h, w_ref[:, o_g:])

    def finish(g, pa, pb, pg):
        r0 = g * n
        _gmlp_rows(pa, lng_ref, lnb_ref, ws_ref, bias_ref, ya_out, r0)
        _attn_prep_rows(pb, cos_ref, sin_ref, again_ref, ared_ref, aexp_ref, q_out, k_out, v_out, r0)
        _dn_prep_rows(xe_ref.at[g], pg, cw_ref, dred_ref, dexp_ref, al_ref, dt_ref, dqkv_out, g_out, r0)

    pending = project(0)
    for g in range(1, ngroups):
        nxt = project(g)
        finish(g - 1, *pending)
        pending = nxt
    finish(ngroups - 1, *pending)


def _row_block_sources(src, tm, ncb, first=0, halo=False):
    parts = src if isinstance(src, tuple) else (src,)
    starts = (0, ncb) if len(parts) == 2 else (0,)
    hb = tm // HALO
    arrays, specs = [], []
    for arr, start in zip(parts, starts):
        d = arr.shape[-1]
        nb = arr.shape[1] // tm

        def blk(i, start=start, nb=nb):
            return jnp.clip(i + first - start, 0, nb - 1)

        arrays.append(arr)
        specs.append(pl.BlockSpec((1, tm, d), lambda b, i, blk=blk: (b, blk(i), 0)))
        if halo:
            arrays += [arr, arr]
            specs.append(pl.BlockSpec((1, HALO, d), lambda b, i, blk=blk: (b, jnp.maximum(blk(i) * hb - 1, 0), 0)))
            specs.append(pl.BlockSpec(
                (1, HALO, d), lambda b, i, blk=blk, nb=nb: (b, jnp.minimum((blk(i) + 1) * hb, nb * hb - 1), 0)))
    return arrays, specs


def _pre_mixer(src, modl, w_in_p, ln_g, ln_b, w_s, b_s, cos_t, sin_t, q_g, k_g, conv_w, a_log, dt_bias,
               tm, ncb, ctx_row):
    parts = src if isinstance(src, tuple) else (src,)
    bsz, _, d = parts[0].shape
    t = sum(p.shape[1] for p in parts)
    nblk = t // tm
    row_arrays, row_specs = _row_block_sources(src, tm, ncb, halo=True)
    nqk = ATT_WIDTH + KV_WIDTH
    bias = jnp.repeat(b_s.T, HEAD_DIM, axis=1)
    gain = jnp.concatenate([jnp.tile(q_g, ATT_Q_HEADS), jnp.tile(k_g, ATT_KV_HEADS)]).reshape(1, nqk)
    ared, aexp = _head_maps(nqk, 1.0 / HEAD_DIM)
    dred, dexp = _head_maps(2 * DN_WIDTH, 1.0)
    pad = lambda v: jnp.zeros((1, LANES), F32).at[0, :2 * DN_HEADS].set(v.reshape(-1))
    mspec = lambda c: pl.BlockSpec((None, None, 1, d), lambda b, i: (jnp.where(i < ncb, ctx_row, b), c, 0, 0))
    rspec = lambda w: pl.BlockSpec((1, tm, w), lambda b, i: (b, i, 0))
    hspec = lambda nh, w: pl.BlockSpec((1, nh, tm, w), lambda b, i: (b, 0, i, 0))
    heads = ((ATT_Q_HEADS, HEAD_DIM), (ATT_KV_HEADS, HEAD_DIM), (ATT_KV_HEADS, LANES))
    rows_out = ((QKV_COLS, F32), (LANES, F32), (DN_WIDTH, BF16))
    return pl.pallas_call(
        functools.partial(_pre_mixer_kernel, tm=tm, ncb=ncb, nblk=nblk, nsrc=len(parts)),
        grid=(bsz, nblk),
        in_specs=row_specs + [
            mspec(0), mspec(1),
            pl.BlockSpec(w_in_p.shape, lambda b, i: (0, 0), pipeline_mode=pl.Buffered(1)),
            _const_spec((1, A_WIDTH)), _const_spec((1, A_WIDTH)), _const_spec(w_s.shape), _const_spec(bias.shape),
            pl.BlockSpec((tm, LANES), lambda b, i: (i, 0)), pl.BlockSpec((tm, LANES), lambda b, i: (i, 0)),
            _const_spec((1, nqk)), _const_spec(ared.shape), _const_spec(aexp.shape),
            _const_spec(conv_w.shape), _const_spec(dred.shape), _const_spec(dexp.shape),
            _const_spec((1, LANES)), _const_spec((1, LANES)),
        ],
        out_specs=[rspec(A_WIDTH)] + [hspec(nh, w) for nh, w in heads] + [rspec(w) for w, _ in rows_out],
        out_shape=[jax.ShapeDtypeStruct((bsz, t, A_WIDTH), BF16)]
        + [jax.ShapeDtypeStruct((bsz, nh, t, w), BF16) for nh, w in heads]
        + [jax.ShapeDtypeStruct((bsz, t, w), dt) for w, dt in rows_out],
        scratch_shapes=[pltpu.VMEM((tm // MLP_CHUNK, MLP_CHUNK + 2 * HALO, QKV_COLS), F32)],
        compiler_params=_params(2),
        name="pre_mixer",
    )(*row_arrays, modl, modl, w_in_p,
      ln_g.reshape(1, -1), ln_b.reshape(1, -1), w_s.astype(BF16), bias,
      cos_t, sin_t, gain, ared, aexp, conv_w, dred, dexp, pad(a_log), pad(dt_bias))


TILE_A = 4 * HEAD_DIM


def _split3(x):
    hi = x.astype(BF16)
    r = x - hi.astype(F32)
    mid = r.astype(BF16)
    lo = (r - mid.astype(F32)).astype(BF16)
    return hi, mid, lo


def _head_id(shape, axis):
    return jnp.right_shift(lax.broadcasted_iota(jnp.int32, shape, axis), int(math.log2(HEAD_DIM)))


class _Packed:
    def __init__(self):
        c, w = DN_CHUNK, DN_WIDTH
        wb = w - TILE_A
        self.mask_a = _head_id((TILE_A, TILE_A), 0) == _head_id((TILE_A, TILE_A), 1)
        self.mask_b = _head_id((wb, wb), 0) == _head_id((wb, wb), 1)
        self.ri = lax.broadcasted_iota(jnp.int32, (c, w), 0)
        self.cj = lax.broadcasted_iota(jnp.int32, (c, w), 1) & (HEAD_DIM - 1)
        self.eye = (self.ri == self.cj).astype(F32)
        self.incl = (self.cj <= self.ri, self.cj >= self.ri)
        self.strict = (self.cj < self.ri, self.cj > self.ri)
        r64 = lax.broadcasted_iota(jnp.int32, (c, c), 0)
        c64 = lax.broadcasted_iota(jnp.int32, (c, c), 1)
        self.order = ((c64 <= r64).astype(BF16), (c64 >= r64).astype(BF16))
        self.blk = {b: jnp.right_shift(self.ri, int(math.log2(b))) == jnp.right_shift(self.cj, int(math.log2(b)))
                    for b in (8, 16, 32)}

    def weights(self, y):
        y16 = y.astype(BF16)
        ya = jnp.concatenate([y16[:, :TILE_A]] * 4, axis=0)
        yb = jnp.concatenate([y16[:, TILE_A:]] * 2, axis=0)
        zero = jnp.zeros((), BF16)
        return jnp.where(self.mask_a, ya, zero), jnp.where(self.mask_b, yb, zero)

    def dot(self, x, y):
        wa, wb = self.weights(y)
        x16 = x.astype(BF16)
        return jnp.concatenate([_dot(x16[:, :TILE_A], wa), _dot(x16[:, TILE_A:], wb)], axis=-1)

    def dot_nt(self, x, y):
        wa, wb = self.weights(y)
        x16 = x.astype(BF16)
        return jnp.concatenate([_dot_nt(x16[:, :TILE_A], wa), _dot_nt(x16[:, TILE_A:], wb)], axis=-1)

    def dot_tn(self, x, y):
        x16, y16 = x.astype(BF16), y.astype(BF16)
        ta = _dot_tn(x16[:, :TILE_A], y16[:, :TILE_A])
        tb = _dot_tn(x16[:, TILE_A:], y16[:, TILE_A:])
        return jnp.where(self.mask_a, ta, 0.0), jnp.where(self.mask_b, tb, 0.0)

    def tri_inverse(self, lows, between=lambda: None):
        c = DN_CHUNK
        nb = [-jnp.where(self.blk[8], low, 0.0) for low in lows]
        x = [self.eye + a for a in nb]
        pw = [self.dot(a, a) for a in nb]
        between()
        st = [self.dot(jnp.concatenate([xa, p], axis=0), p) for xa, p in zip(x, pw)]
        between()
        x = [xa + s[:c] for xa, s in zip(x, st)]
        x = [xa + self.dot(xa, s[c:]) for xa, s in zip(x, st)]
        between()
        for b in (8, 16, 32):
            inner = self.blk[b]
            outer = self.blk[2 * b] if 2 * b in self.blk else None
            off = jnp.logical_not(inner) if outer is None else jnp.logical_and(outer, jnp.logical_not(inner))
            xc = [self.dot(xa, jnp.where(off, low, 0.0)) for xa, low in zip(x, lows)]
            between()
            x = [xa - self.dot(t, xa) for xa, t in zip(x, xc)]
            between()
        return x


def _spread_heads(x, first):
    c = x.shape[0]
    low_half = lax.broadcasted_iota(jnp.int32, (c, LANES), 1) < HEAD_DIM
    tiles = []
    for j in range(0, DN_HEADS, 2):
        a = jnp.broadcast_to(x[:, first + j:first + j + 1], (c, LANES))
        b = jnp.broadcast_to(x[:, first + j + 1:first + j + 2], (c, LANES))
        tiles.append(jnp.where(low_half, a, b))
    return jnp.concatenate(tiles, axis=-1)


def _dn_chunk_terms(pk, dirs, q, k, v, gates, between=lambda: None):
    c = DN_CHUNK
    incl = [pk.incl[d] for d in dirs]
    strict = [pk.strict[d] for d in dirs]
    gc = [sum(_dot(pk.order[d], p) for p in _split3(g)) for d, g in zip(dirs, gates)]
    gcx = [_spread_heads(a, d * DN_HEADS) for d, a in zip(dirs, gc)]
    bx = [_spread_heads(g, (2 + d) * DN_HEADS) for d, g in zip(dirs, gates)]
    between()
    gtot = [a[c - 1:c, :] if d == 0 else a[0:1, :] for d, a in zip(dirs, gcx)]
    grow = [jnp.sum(pk.eye * a, 0, keepdims=True) for a in gcx]
    decay = [jnp.where(m, jnp.exp(jnp.minimum(a - r, 0.0)), 0.0) for m, a, r in zip(incl, gcx, grow)]
    between()
    kbeta = [a * b for a, b in zip(k, bx)]
    kq = [pk.dot_nt(jnp.concatenate([a, b], axis=0), kk) for a, b, kk in zip(kbeta, q, k)]
    between()
    low = [jnp.where(m, a[:c] * dc, 0.0) for m, a, dc in zip(strict, kq, decay)]
    intra = [jnp.where(m, a[c:] * dc, 0.0) for m, a, dc in zip(incl, kq, decay)]
    between()
    tinv = pk.tri_inverse(low, between)
    eg = [jnp.exp(a) for a in gcx]
    u = [pk.dot(t, a * b) for t, a, b in zip(tinv, v, bx)]
    wm = [pk.dot(t, a * e) for t, a, e in zip(tinv, kbeta, eg)]
    lhs = [jnp.concatenate([a * e, w_], axis=0).astype(BF16) for a, e, w_ in zip(q, eg, wm)]
    k_dec = [a * jnp.exp(gt - gx) for a, gt, gx in zip(k, gtot, gcx)]
    egl = [jnp.exp(gt) for gt in gtot]
    return [dict(lhs=a, u=b, intra=i, k_dec=kd, egl=e) for a, b, i, kd, e in zip(lhs, u, intra, k_dec, egl)]


def _dn_scan_kernel(xf, gf, xb, gb, of, ob, sa_ref, sb_ref, *, nsub):
    c, w = DN_CHUNK, DN_WIDTH

    @pl.when(pl.program_id(1) == 0)
    def _():
        sa_ref[...] = jnp.zeros_like(sa_ref)
        sb_ref[...] = jnp.zeros_like(sb_ref)

    pk = _Packed()
    refs = ((xf, gf, of), (xb, gb, ob))
    items = [(d, j if d == 0 else nsub - 1 - j) for j in range(nsub) for d in (0, 1)]
    rows = [slice(j * c, (j + 1) * c) for _, j in items]
    dirs = [d for d, _ in items]
    terms = _dn_chunk_terms(
        pk, dirs,
        [refs[d][0][0, r, 0:w] for d, r in zip(dirs, rows)],
        [refs[d][0][0, r, w:2 * w] for d, r in zip(dirs, rows)],
        [refs[d][0][0, r, 2 * w:3 * w] for d, r in zip(dirs, rows)],
        [refs[d][1][0, r, :] for d, r in zip(dirs, rows)])
    state = [(sa_ref[d], sb_ref[d]) for d in (0, 1)]
    for d, r, t in zip(dirs, rows, terms):
        sa, sb = state[d]
        lhs = t["lhs"]
        res = jnp.concatenate([_dot(lhs[:, :TILE_A], sa.astype(BF16)), _dot(lhs[:, TILE_A:], sb.astype(BF16))],
                              axis=-1)
        v_new = t["u"] - res[c:]
        refs[d][2][0, r, :] = res[:c] + pk.dot(t["intra"], v_new)
        ta, tb = pk.dot_tn(t["k_dec"], v_new)
        state[d] = (sa * t["egl"][:, :TILE_A] + ta, sb * t["egl"][:, TILE_A:] + tb)
    for d in (0, 1):
        sa_ref[d], sb_ref[d] = state[d]


DN_BLOCKS = (4 * DN_CHUNK, 2 * DN_CHUNK)


def _dn_scan(qkv, g, tc):
    bsz, t, w3 = qkv.shape
    w = w3 // 3
    c = next(blk for blk in DN_BLOCKS if tc % blk == 0 and t % blk == 0)
    nct, ncx = t // c, tc // c

    def bwd_blk(s):
        return jnp.where(s < ncx, ncx - 1 - s, nct - 1 - s + ncx)

    fspec = lambda width: pl.BlockSpec((1, c, width), lambda b, s: (b, s, 0))
    bspec = lambda width: pl.BlockSpec((1, c, width), lambda b, s: (b, bwd_blk(s), 0))
    return pl.pallas_call(
        functools.partial(_dn_scan_kernel, nsub=c // DN_CHUNK),
        grid=(bsz, nct),
        in_specs=[fspec(w3), fspec(LANES), bspec(w3), bspec(LANES)],
        out_specs=[fspec(w), bspec(w)],
        out_shape=[jax.ShapeDtypeStruct((bsz, t, w), F32)] * 2,
        scratch_shapes=[pltpu.VMEM((2, TILE_A, TILE_A), F32), pltpu.VMEM((2, w - TILE_A, w - TILE_A), F32)],
        compiler_params=_params(2),
        name="dn_scan",
    )(qkv, g, qkv, g)


def _post_kernel(*refs, alpha, f_chunk, nsplit, nsrc, ncb):
    x_refs, rest = refs[:nsrc], refs[nsrc:]
    (ya_ref, yb0_ref, yb1_ref, of_ref, ob_ref, z_ref, ngain_ref, nred_ref, nexp_ref,
     g1_ref, sh2_ref, sc2_ref, g2_ref, wa_ref, wb0_ref, wb1_ref, wc_ref, l1g_ref, l1b_ref, l2g_ref, l2b_ref,
     wup_ref, wdn_ref, o_ref) = rest
    tm = o_ref.shape[1]
    groups = [slice(r * tm // nsplit, (r + 1) * tm // nsplit) for r in range(nsplit)]
    is_ctx = pl.program_id(1) < ncb

    def x_rows(r):
        return x_refs[0][0, r, :] if nsrc == 1 else jnp.where(is_ctx, x_refs[0][0, r, :], x_refs[1][0, r, :])

    o = [of_ref[0, r, :] + ob_ref[0, r, :] for r in groups]
    yc = [(a * _head_rsqrt(a, nred_ref, nexp_ref) * ngain_ref[...]
           * _silu(z_ref[0, r, :].astype(F32))).astype(BF16) for a, r in zip(o, groups)]
    branch = [(_dot(ya_ref[0, r, :], wa_ref[...]) + _dot(yb0_ref[0, 0, r, :], wb0_ref[...])
               + _dot(yb1_ref[0, 0, r, :], wb1_ref[...]) + _dot(c, wc_ref[...])) for c, r in zip(yc, groups)]
    x1 = [_norm(alpha * x_rows(r) + g1_ref[...] * br) * l1g_ref[...] + l1b_ref[...]
          for br, r in zip(branch, groups)]
    h = [(_norm(a) * (1.0 + sc2_ref[...]) + sh2_ref[...]).astype(BF16) for a in x1]
    d_ff = wup_ref.shape[1]
    m = [None] * nsplit
    for f in range(0, d_ff, f_chunk):
        up = [jnp.maximum(_dot(a, wup_ref[:, f:f + f_chunk]), 0.0) for a in h]
        part = [_dot((a * a).astype(BF16), wdn_ref[f:f + f_chunk, :]) for a in up]
        m = [p if acc is None else acc + p for acc, p in zip(m, part)]
    for r, a, mm in zip(groups, x1, m):
        o_ref[0, r, :] = _norm(alpha * a + g2_ref[...] * mm) * l2g_ref[...] + l2b_ref[...]


def _post(src, ya, yb, o_f, o_b, pz, norm_g, modl, w_out, ln1_g, ln1_b, ln2_g, ln2_b, w_up, w_down,
          tm, ncb, ctx_row, alpha, need_ctx):
    parts = src if isinstance(src, tuple) else (src,)
    bsz, _, d = parts[0].shape
    t = sum(p.shape[1] for p in parts)
    assert need_ctx or len(parts) == 1
    ngain = jnp.tile(norm_g, DN_HEADS).reshape(1, DN_WIDTH)
    nred, nexp = _head_maps(DN_WIDTH, 1.0 / HEAD_DIM)
    first = 0 if need_ctx else ncb
    yb_first = first - (t - yb.shape[2]) // tm
    gw = ATT_GROUP * HEAD_DIM
    o1 = A_WIDTH
    wa, wb0, wb1, wc = w_out[:o1], w_out[o1:o1 + gw], w_out[o1 + gw:o1 + 2 * gw], w_out[o1 + 2 * gw:]
    mspec = lambda c: pl.BlockSpec(
        (None, None, 1, d), lambda b, i: (jnp.where(i + first < ncb, ctx_row, b), c, 0, 0))
    rspec = lambda w: pl.BlockSpec((1, tm, w), lambda b, i: (b, i + first, 0))
    vec = lambda a: a.reshape(1, d)
    wspec = lambda a: pl.BlockSpec(a.shape, lambda b, i: (0, 0), pipeline_mode=pl.Buffered(1))
    weights = [w.astype(BF16) for w in (wa, wb0, wb1, wc)]
    w_up, w_down = w_up.astype(BF16), w_down.astype(BF16)
    x_arrays, x_specs = _row_block_sources(src, tm, ncb, first=first)
    return pl.pallas_call(
        functools.partial(_post_kernel, alpha=alpha, f_chunk=2048, nsplit=2, nsrc=len(parts), ncb=ncb),
        grid=(bsz, t // tm - first),
        in_specs=x_specs + [
            rspec(A_WIDTH),
            pl.BlockSpec((1, 1, tm, gw), lambda b, i: (b, 0, i + yb_first, 0)),
            pl.BlockSpec((1, 1, tm, gw), lambda b, i: (b, 1, i + yb_first, 0)),
            rspec(DN_WIDTH), rspec(DN_WIDTH), rspec(DN_WIDTH),
            _const_spec(ngain.shape), _const_spec(nred.shape), _const_spec(nexp.shape),
            mspec(2), mspec(3), mspec(4), mspec(5),
            *[wspec(w) for w in weights],
            _const_spec((1, d)), _const_spec((1, d)), _const_spec((1, d)), _const_spec((1, d)),
            wspec(w_up), wspec(w_down),
        ],
        out_specs=pl.BlockSpec((1, tm, d), lambda b, i: (b, i, 0)),
        out_shape=jax.ShapeDtypeStruct((bsz, t - first * tm, d), F32),
        compiler_params=_params(2),
        name="post_mixer",
    )(*x_arrays, ya, yb, yb, o_f, o_b, pz, ngain, nred, nexp, modl, modl, modl, modl, *weights,
      vec(ln1_g), vec(ln1_b), vec(ln2_g), vec(ln2_b), w_up, w_down)


def _rope_tables(tc, tl):
    pos = jnp.arange(tl)
    row = (pos // GRID_W).astype(F32)
    col = (pos % GRID_W).astype(F32)
    half = HEAD_DIM // 2
    inv = 1.0 / (ROPE_THETA ** (jnp.arange(0, half, 2, dtype=F32) / half))
    ar, ac = row[:, None] * inv, col[:, None] * inv
    cos = jnp.concatenate([jnp.cos(ar), jnp.cos(ar), jnp.cos(ac), jnp.cos(ac)], -1)
    sin = jnp.concatenate([-jnp.sin(ar), jnp.sin(ar), -jnp.sin(ac), jnp.sin(ac)], -1)
    cos = jnp.concatenate([jnp.ones((tc, HEAD_DIM), F32), cos], 0)
    sin = jnp.concatenate([jnp.zeros((tc, HEAD_DIM), F32), sin], 0)
    scale = HEAD_DIM ** -0.5 * math.log2(math.e)
    return jnp.concatenate([cos * scale, cos], -1), jnp.concatenate([sin * scale, sin], -1)


def kernel(x, c, ctx, c_ctx, mod_w, mod_b, w_in, w_out, gmlp_ln_g, gmlp_ln_b, gmlp_w_s, gmlp_b_s,
           attn_q_g, attn_k_g, dn_conv_w, dn_a_log, dn_dt_bias, dn_norm_g,
           ln1_g, ln1_b, ln2_g, ln2_b, w_up, w_down):
    bsz, tl, d = x.shape
    tc = ctx.shape[1]
    depth = mod_w.shape[0]
    assert bsz < MOD_ROWS and tl % GRID_W == 0 and tc % MLP_CHUNK == 0 and tl % MLP_CHUNK == 0
    tm = 256 if tc % 256 == 0 else MLP_CHUNK
    ncb = tc // tm
    ctx_row = bsz
    alpha = (2 * depth) ** 0.25

    cs = jnp.zeros((MOD_ROWS, d), F32).at[:bsz].set(c).at[ctx_row].set(c_ctx)
    mod = _modulation(cs, mod_w, mod_b).reshape(depth, MOD_ROWS, 6, 1, d)
    cos_t, sin_t = _rope_tables(tc, tl)
    xcat = (ctx, x) if depth > 1 else jnp.concatenate([ctx, x], axis=1)

    for l in range(depth):
        need_ctx = l < depth - 1
        w_in_p = jnp.pad(w_in[l], ((0, 0), (0, IN_COLS_PAD - IN_COLS))).astype(BF16)
        ya, qa, ka, va, dqkv, dg, pz = _pre_mixer(
            xcat, mod[l], w_in_p, gmlp_ln_g[l], gmlp_ln_b[l], gmlp_w_s[l], gmlp_b_s[l], cos_t, sin_t,
            attn_q_g[l], attn_k_g[l], dn_conv_w[l], dn_a_log[l], dn_dt_bias[l], tm, ncb, ctx_row)
        yb_ctx, yb = _attention(qa, ka, va, tc, need_ctx)
        if need_ctx:
            yb = jnp.concatenate([yb_ctx, yb], axis=2)
        o_f, o_b = _dn_scan(dqkv, dg, tc)
        xcat = _post(xcat, ya, yb, o_f, o_b, pz, dn_norm_g[l], mod[l], w_out[l], ln1_g[l], ln1_b[l], ln2_g[l], ln2_b[l],
                     w_up[l], w_down[l], tm, ncb, ctx_row, alpha, need_ctx)
    return xcat
```
